```python
import math
import jax, jax.numpy as jnp
from jax import lax
import numpy as np

D_MODEL = 2048
BATCH = 8
SEQ = 2048
DEPTH = 4

RET_HEADS = 8
RET_DK = 128
RET_DV = 128
RET_CHUNK = 128
GN_EPS = 1e-5
ML_HEADS = 8
ML_DK = 64
ML_DV = 128
ML_CHUNK = 128
GATE_SOFTCAP = 15.0
MB_HEADS = 8
MB_DH = 128
MB_BLOCK = 256
MB_TOPK = 3
MB_QCHUNK = 32
ROPE_THETA = 10000.0
D_FF = 5632
FFN_CONV = 3
EPS = 1e-6

RET_W = RET_HEADS * RET_DV
ML_W = ML_HEADS * ML_DV
MB_W = MB_HEADS * MB_DH
SPLIT_SIZES = (
    RET_HEADS * RET_DK, RET_HEADS * RET_DK, RET_W, RET_W,
    ML_HEADS * ML_DK, ML_HEADS * ML_DK, ML_W, ML_W, ML_HEADS, ML_HEADS,
    MB_W, MB_W, MB_W,
    D_MODEL, D_MODEL, D_MODEL,
)
N_IN = sum(SPLIT_SIZES)
SPLIT_IDX = tuple(int(s) for s in np.cumsum(SPLIT_SIZES)[:-1])

kernel_name = 'hybrid_retention_mlstm_moba_convglu'


def rms_norm(x, g):
    xf = x.astype(jnp.float32)
    y = xf * lax.rsqrt(jnp.mean(xf * xf, axis=-1, keepdims=True) + EPS)
    return (y * g.astype(jnp.float32)).astype(x.dtype)


def retnet_inv_freq():
    return 1.0 / (ROPE_THETA ** jnp.linspace(0.0, 1.0, RET_DK // 2, dtype=jnp.float32))


def rope_inv_freq():
    return 1.0 / (ROPE_THETA ** (jnp.arange(0, MB_DH, 2, dtype=jnp.float32) / MB_DH))


def rotary(x, inv_freq):
    S = x.shape[1]
    ang = jnp.arange(S, dtype=jnp.float32)[:, None] * inv_freq[None, :]
    cos = jnp.cos(ang)[None, :, None, :]
    sin = jnp.sin(ang)[None, :, None, :]
    x1, x2 = jnp.split(x.astype(jnp.float32), 2, axis=-1)
    return jnp.concatenate([x1 * cos - x2 * sin, x1 * sin + x2 * cos], axis=-1).astype(x.dtype)


def retention(q, k, v, g, gn_w):
    B, S = q.shape[0], q.shape[1]
    H, C, N = RET_HEADS, RET_CHUNK, S // RET_CHUNK
    f32 = jnp.float32
    qc = q.astype(f32).reshape(B, N, C, H, RET_DK)
    kc = (k.astype(f32) * RET_DK ** -0.5).reshape(B, N, C, H, RET_DK)
    vc = v.astype(f32).reshape(B, N, C, H, RET_DV)
    log_gamma = jnp.log1p(-jnp.exp2(-5.0 - jnp.arange(H, dtype=f32)))
    pos = jnp.arange(C, dtype=f32)
    diff = pos[:, None] - pos[None, :]
    decay = jnp.where(diff >= 0, jnp.exp(log_gamma[:, None, None] * jnp.maximum(diff, 0.0)), 0.0)
    scores = jnp.einsum('bnihd,bnjhd->bnhij', qc, kc) * decay
    intra = jnp.einsum('bnhij,bnjhe->bnihe', scores, vc)
    k_w = kc * jnp.exp((C - 1.0 - pos)[:, None] * log_gamma[None, :])[None, None, :, :, None]
    kv = jnp.einsum('bnjhd,bnjhe->bnhde', k_w, vc)
    chunk_decay = jnp.exp(C * log_gamma)[None, :, None, None]

    def step(state, kv_n):
        return state * chunk_decay + kv_n, state

    _, prev = lax.scan(step, jnp.zeros((B, H, RET_DK, RET_DV), f32), jnp.moveaxis(kv, 1, 0))
    prev = jnp.moveaxis(prev, 0, 1)
    q_w = qc * jnp.exp((pos + 1.0)[:, None] * log_gamma[None, :])[None, None, :, :, None]
    inter = jnp.einsum('bnihd,bnhde->bnihe', q_w, prev)
    o = (intra + inter).reshape(B, S, H, RET_DV)
    o = o - o.mean(axis=-1, keepdims=True)
    o = o * lax.rsqrt(jnp.mean(o * o, axis=-1, keepdims=True) + GN_EPS)
    o = o.reshape(B, S, H * RET_DV) * gn_w.astype(f32)
    return (jax.nn.silu(g.astype(f32)) * o).astype(q.dtype)


def softcap(x):
    return GATE_SOFTCAP * jnp.tanh(x / GATE_SOFTCAP)


def mlstm(q, k, v, o_pre, i_pre, f_pre, norm_w):
    B, S = q.shape[0], q.shape[1]
    H, C, N = ML_HEADS, ML_CHUNK, S // ML_CHUNK
    f32 = jnp.float32
    qc = q.astype(f32).reshape(B, N, C, H, ML_DK)
    kc = (k.astype(f32) * ML_DK ** -0.5).reshape(B, N, C, H, ML_DK)
    vc = v.astype(f32).reshape(B, N, C, H, ML_DV)
    i_log = softcap(i_pre.astype(f32)).reshape(B, N, C, H).transpose(0, 1, 3, 2)
    f_log = jax.nn.log_sigmoid(softcap(f_pre.astype(f32))).reshape(B, N, C, H).transpose(0, 1, 3, 2)
    a = jnp.cumsum(f_log, axis=-1)
    a_last = a[..., -1]
    causal = jnp.tril(jnp.ones((C, C), dtype=bool))
    d_log = jnp.where(causal, a[..., :, None] - a[..., None, :] + i_log[..., None, :], -jnp.inf)
    w_end = a_last[..., None] - a + i_log
    g_loc = jnp.max(w_end, axis=-1)
    w_exp = jnp.exp(w_end - g_loc[..., None])
    kv_loc = jnp.einsum('bnhj,bnjhd,bnjhe->bnhde', w_exp, kc, vc)
    n_loc = jnp.einsum('bnhj,bnjhd->bnhd', w_exp, kc)

    def step(carry, xs):
        c_st, n_st, m_st = carry
        kv_n, nn_n, g_n, a_n = xs
        m_new = jnp.maximum(a_n + m_st, g_n)
        s_old = jnp.exp(a_n + m_st - m_new)
        s_new = jnp.exp(g_n - m_new)
        c_new = s_old[..., None, None] * c_st + s_new[..., None, None] * kv_n
        n_new = s_old[..., None] * n_st + s_new[..., None] * nn_n
        return (c_new, n_new, m_new), (c_st, n_st, m_st)

    init = (jnp.zeros((B, H, ML_DK, ML_DV), f32), jnp.zeros((B, H, ML_DK), f32), jnp.zeros((B, H), f32))
    xs = (jnp.moveaxis(kv_loc, 1, 0), jnp.moveaxis(n_loc, 1, 0), jnp.moveaxis(g_loc, 1, 0), jnp.moveaxis(a_last, 1, 0))
    _, (c_prev, n_prev, m_prev) = lax.scan(step, init, xs)
    c_prev = jnp.moveaxis(c_prev, 0, 1)
    n_prev = jnp.moveaxis(n_prev, 0, 1)
    m_prev = jnp.moveaxis(m_prev, 0, 1)
    inter_log = a + m_prev[..., None]
    m_row = jnp.maximum(jnp.max(d_log, axis=-1), inter_log)
    qk = jnp.einsum('bnihd,bnjhd->bnhij', qc, kc) * jnp.exp(d_log - m_row[..., None])
    s_inter = jnp.exp(inter_log - m_row)
    num = jnp.einsum('bnhij,bnjhe->bnihe', qk, vc) + jnp.einsum('bnihd,bnhde->bnihe', qc, c_prev) * jnp.swapaxes(s_inter, 2, 3)[..., None]
    den = qk.sum(axis=-1) + jnp.einsum('bnihd,bnhd->bnhi', qc, n_prev) * s_inter
    den = jnp.maximum(jnp.abs(den), jnp.exp(-m_row))
    h = (num / jnp.swapaxes(den, 2, 3)[..., None]).reshape(B, S, H, ML_DV)
    h = h * lax.rsqrt(jnp.mean(h * h, axis=-1, keepdims=True) + EPS) * norm_w.astype(f32).reshape(H, ML_DV)
    return (jax.nn.sigmoid(o_pre.astype(f32)) * h.reshape(B, S, H * ML_DV)).astype(q.dtype)


def moba(q, k, v):
    B, S, H, D = q.shape
    nb = -(-S // MB_BLOCK)
    s_pad = nb * MB_BLOCK
    pad = ((0, 0), (0, s_pad - S), (0, 0), (0, 0))
    qh = jnp.pad(q, pad).transpose(0, 2, 1, 3)
    kb = jnp.pad(k, pad).transpose(0, 2, 1, 3).reshape(B, H, nb, MB_BLOCK, D)
    vb = jnp.pad(v, pad).transpose(0, 2, 1, 3).reshape(B, H, nb, MB_BLOCK, D)
    k_mean = kb.astype(jnp.float32).mean(axis=3)
    gate = jnp.einsum('bhsd,bhnd->bhsn', qh.astype(jnp.float32), k_mean)
    q_blk = jnp.arange(s_pad) // MB_BLOCK
    gate = jnp.where(jnp.arange(nb)[None, :] < q_blk[:, None], gate, -jnp.inf)
    topk = min(MB_TOPK, nb)
    _, idx = lax.top_k(gate, topk)
    valid = jnp.arange(topk)[None, :] < q_blk[:, None]
    n_qc = s_pad // MB_QCHUNK
    scale = D ** -0.5
    q_x = qh.reshape(B, H, n_qc, MB_QCHUNK, D).transpose(2, 0, 1, 3, 4)
    idx_x = idx.reshape(B, H, n_qc, MB_QCHUNK, topk).transpose(2, 0, 1, 3, 4)
    valid_x = valid.reshape(n_qc, MB_QCHUNK, topk)
    bi = jnp.arange(B)[:, None, None]
    hi = jnp.arange(H)[None, :, None]

    def attend(xs):
        ci, qc, ic, vm = xs
        blk = (ci * MB_QCHUNK) // MB_BLOCK
        qpos = ci * MB_QCHUNK + jnp.arange(MB_QCHUNK)
        kpos = blk * MB_BLOCK + jnp.arange(MB_BLOCK)
        k_own = lax.dynamic_index_in_dim(kb, blk, axis=2, keepdims=False)
        v_own = lax.dynamic_index_in_dim(vb, blk, axis=2, keepdims=False)
        s = jnp.einsum('bhqd,bhkd->bhqk', qc, k_own, preferred_element_type=jnp.float32) * scale
        s = jnp.where(kpos[None, :] <= qpos[:, None], s, -jnp.inf)
        m = jnp.max(s, axis=-1)
        p = jnp.exp(s - m[..., None])
        l = p.sum(axis=-1)
        acc = jnp.einsum('bhqk,bhkd->bhqd', p, v_own.astype(jnp.float32))
        for slot in range(topk):
            sel = ic[..., slot]
            k_s = kb[bi, hi, sel]
            v_s = vb[bi, hi, sel]
            sc = jnp.einsum('bhqd,bhqkd->bhqk', qc, k_s, preferred_element_type=jnp.float32) * scale
            sc = jnp.where(vm[:, slot][None, None, :, None], sc, -jnp.inf)
            m_new = jnp.maximum(m, jnp.max(sc, axis=-1))
            alpha = jnp.exp(m - m_new)
            p = jnp.exp(sc - m_new[..., None])
            l = alpha * l + p.sum(axis=-1)
            acc = alpha[..., None] * acc + jnp.einsum('bhqk,bhqkd->bhqd', p, v_s.astype(jnp.float32))
            m = m_new
        return (acc / l[..., None]).astype(q.dtype)

    out = lax.map(attend, (jnp.arange(n_qc), q_x, idx_x, valid_x))
    out = out.transpose(1, 0, 3, 2, 4).reshape(B, s_pad, H, D)
    return out[:, :S]


def conv_glu_ffn(h, w_up, conv_w, conv_b, w_down):
    S = h.shape[1]
    u = h @ w_up
    up = jnp.pad(u, ((0, 0), (FFN_CONV - 1, 0), (0, 0)))
    u = conv_b + sum(up[:, j:j + S] * conv_w[j] for j in range(FFN_CONV))
    a, b = jnp.split(u, 2, axis=-1)
    return (jax.nn.silu(a) * b) @ w_down


def setup_inputs(seed: int = 0) -> dict:
    key = jax.random.key(seed)
    ks = jax.random.split(key, 20)
    f32 = jnp.float32

    def nrm(k, shape, scale):
        return jax.random.normal(k, shape, f32) * scale

    return {
        'x': nrm(ks[0], (BATCH, SEQ, D_MODEL), 1.0),
        'w_in': nrm(ks[1], (DEPTH, D_MODEL, N_IN), D_MODEL ** -0.5),
        'b_ig': nrm(ks[2], (DEPTH, ML_HEADS), 0.1),
        'b_fg': jnp.linspace(3.0, 6.0, ML_HEADS, dtype=f32)[None, :] + nrm(ks[3], (DEPTH, ML_HEADS), 0.1),
        'norm_mix': 1.0 + nrm(ks[4], (DEPTH, D_MODEL), 0.1),
        'ret_gn': 1.0 + nrm(ks[5], (DEPTH, RET_W), 0.1),
        'ml_norm': 1.0 + nrm(ks[6], (DEPTH, ML_W), 0.1),
        'q_norm': 1.0 + nrm(ks[7], (DEPTH, MB_DH), 0.1),
        'k_norm': 1.0 + nrm(ks[8], (DEPTH, MB_DH), 0.1),
        'w_pa': nrm(ks[9], (DEPTH, RET_W, D_MODEL), RET_W ** -0.5),
        'w_pb': nrm(ks[10], (DEPTH, ML_W, D_MODEL), ML_W ** -0.5),
        'w_pc': nrm(ks[11], (DEPTH, MB_W, D_MODEL), MB_W ** -0.5),
        'w_out': nrm(ks[12], (DEPTH, D_MODEL, D_MODEL), D_MODEL ** -0.5),
        'norm_ffn': 1.0 + nrm(ks[13], (DEPTH, D_MODEL), 0.1),
        'w_up': nrm(ks[14], (DEPTH, D_MODEL, 2 * D_FF), D_MODEL ** -0.5),
        'conv_w': nrm(ks[15], (DEPTH, FFN_CONV, 2 * D_FF), FFN_CONV ** -0.5),
        'conv_b': nrm(ks[16], (DEPTH, 2 * D_FF), 0.01),
        'w_down': nrm(ks[17], (DEPTH, D_FF, D_MODEL), D_FF ** -0.5),
    }


def reference(x, w_in, b_ig, b_fg, norm_mix, ret_gn, ml_norm, q_norm, k_norm, w_pa, w_pb, w_pc, w_out, norm_ffn, w_up, conv_w, conv_b, w_down):
    B, S = x.shape[0], x.shape[1]
    ret_freq = retnet_inv_freq()
    rope_freq = rope_inv_freq()
    for l in range(DEPTH):
        h = rms_norm(x, norm_mix[l])
        z = h @ w_in[l]
        (rq, rk, rv, rg, mq, mk, mv, mo, mi, mf, aq, ak, av, ga, gb, gc) = jnp.split(z, SPLIT_IDX, axis=-1)
        rq = rotary(rq.reshape(B, S, RET_HEADS, RET_DK), ret_freq)
        rk = rotary(rk.reshape(B, S, RET_HEADS, RET_DK), ret_freq)
        y_a = retention(rq, rk, rv.reshape(B, S, RET_HEADS, RET_DV), rg, ret_gn[l])
        y_b = mlstm(mq.reshape(B, S, ML_HEADS, ML_DK), mk.reshape(B, S, ML_HEADS, ML_DK),
                    mv.reshape(B, S, ML_HEADS, ML_DV), mo, mi + b_ig[l], mf + b_fg[l], ml_norm[l])
        aq = rotary(rms_norm(aq.reshape(B, S, MB_HEADS, MB_DH), q_norm[l]), rope_freq)
        ak = rotary(rms_norm(ak.reshape(B, S, MB_HEADS, MB_DH), k_norm[l]), rope_freq)
        y_c = moba(aq, ak, av.reshape(B, S, MB_HEADS, MB_DH)).reshape(B, S, MB_W)
        merged = (jax.nn.sigmoid(ga) * (y_a @ w_pa[l]) + jax.nn.sigmoid(gb) * (y_b @ w_pb[l])
                  + jax.nn.sigmoid(gc) * (y_c @ w_pc[l]))
        x = x + merged @ w_out[l]
        x = x + conv_glu_ffn(rms_norm(x, norm_ffn[l]), w_up[l], conv_w[l], conv_b[l], w_down[l])
    return x
```

```python
import functools

import jax
import jax.numpy as jnp
from jax import lax
from jax.experimental import pallas as pl
from jax.experimental.pallas import tpu as pltpu

F32 = jnp.float32
BF16 = jnp.bfloat16

D_MODEL = 2048
DEPTH = 4
RET_HEADS = 8
RET_DK = 128
RET_DV = 128
RET_CHUNK = 128
GN_EPS = 1e-5
ML_HEADS = 8
ML_DK = 64
ML_DV = 128
ML_CHUNK = 128
GATE_SOFTCAP = 15.0
MB_HEADS = 8
MB_DH = 128
MB_BLOCK = 256
MB_TOPK = 3
ROPE_THETA = 10000.0
D_FF = 5632
FFN_CONV = 3
EPS = 1e-6

LANES = 128
N_MAIN = 16384
CB_RQ, CB_RK, CB_RV, CB_RG = 0, 8, 16, 24
CB_MQK, CB_MV, CB_MO = 32, 40, 48
CB_AQ, CB_AK, CB_AV = 56, 64, 72
COL_GA, COL_GB, COL_GC = 10240, 12288, 14336
NEG_BIG = -1e30

VMEM_LIMIT_BYTES = 56 * 1024 * 1024


def _cparams(*sem):
    return pltpu.CompilerParams(dimension_semantics=sem, vmem_limit_bytes=VMEM_LIMIT_BYTES)


def _rms(x, g, eps):
    return x * lax.rsqrt(jnp.mean(x * x, axis=-1, keepdims=True) + eps) * g


def _sigmoid(x):
    return 1.0 / (1.0 + jnp.exp(-x))


def _rot(x, cos, sin_signed):
    return x * cos + pltpu.roll(x, LANES // 2, 1) * sin_signed


def _inproj_kernel(x_ref, g_ref, w_ref, wg_ref, z_ref, zg_ref, h_ref):
    @pl.when(pl.program_id(1) == 0)
    def _():
        hb = _rms(x_ref[...], g_ref[...], EPS).astype(BF16)
        h_ref[...] = hb
        zg_ref[...] = jnp.dot(hb, wg_ref[...], preferred_element_type=F32)

    z_ref[...] = jnp.dot(h_ref[...], w_ref[...], preferred_element_type=F32).astype(BF16)


def _in_proj(x2, norm_w, w_main, w_gate, l, tm, tn):
    T, D = x2.shape
    N = w_main.shape[-1]
    return pl.pallas_call(
        _inproj_kernel,
        grid=(T // tm, N // tn),
        in_specs=[
            pl.BlockSpec((tm, D), lambda i, j: (i, 0)),
            pl.BlockSpec((None, 1, D), lambda i, j: (l, 0, 0)),
            pl.BlockSpec((None, D, tn), lambda i, j: (l, 0, j)),
            pl.BlockSpec((None, D, LANES), lambda i, j: (l, 0, 0)),
        ],
        out_specs=[
            pl.BlockSpec((tm, tn), lambda i, j: (i, j)),
            pl.BlockSpec((tm, LANES), lambda i, j: (i, 0)),
        ],
        out_shape=[jax.ShapeDtypeStruct((T, N), BF16), jax.ShapeDtypeStruct((T, LANES), F32)],
        scratch_shapes=[pltpu.VMEM((tm, D), BF16)],
        compiler_params=_cparams("parallel", "arbitrary"),
        name="in_proj",
    )(x2, norm_w, w_main, w_gate)


def _ret_kernel(q_ref, k_ref, v_ref, g_ref, cos_ref, sin_ref, lg_ref, gn_ref, o_ref):
    S = q_ref.shape[0]
    C = RET_CHUNK
    lg = lg_ref[...]
    ii = lax.broadcasted_iota(jnp.int32, (C, C), 0)
    jj = lax.broadcasted_iota(jnp.int32, (C, C), 1)
    decay = jnp.where(ii >= jj, jnp.exp(lg * jnp.maximum(ii - jj, 0).astype(F32)), 0.0)
    pos = lax.broadcasted_iota(jnp.int32, (C, LANES), 0).astype(F32)
    kdec = jnp.exp((C - 1.0 - pos) * lg)
    qdec = jnp.exp((pos + 1.0) * lg)
    cdec = jnp.exp(float(C) * lg)
    gn = gn_ref[...]

    def body(n, state):
        r = pl.ds(pl.multiple_of(n * C, C), C)
        cs = cos_ref[r, :]
        sn = sin_ref[r, :]
        q = _rot(q_ref[r, :].astype(F32), cs, sn)
        k = _rot(k_ref[r, :].astype(F32), cs, sn) * (RET_DK ** -0.5)
        v = v_ref[r, :]
        s = lax.dot_general(q.astype(BF16), k.astype(BF16), (((1,), (1,)), ((), ())),
                            preferred_element_type=F32) * decay
        intra = jnp.dot(s.astype(BF16), v, preferred_element_type=F32)
        inter = jnp.dot((q * qdec).astype(BF16), state.astype(BF16), preferred_element_type=F32)
        kv = lax.dot_general((k * kdec).astype(BF16), v, (((0,), (0,)), ((), ())),
                             preferred_element_type=F32)
        o = intra + inter
        oc = o - jnp.mean(o, axis=-1, keepdims=True)
        on = oc * lax.rsqrt(jnp.mean(oc * oc, axis=-1, keepdims=True) + GN_EPS) * gn
        g = g_ref[r, :].astype(F32)
        o_ref[r, :] = (g * _sigmoid(g) * on).astype(BF16)
        return state * cdec + kv

    lax.fori_loop(0, S // C, body, jnp.zeros((RET_DK, RET_DV), F32))


def _retention(z, cos, sin, lg, gn, l, B, S):
    T = B * S
    blk = lambda cb: pl.BlockSpec((S, LANES), lambda b, h: (b, cb + h))
    return pl.pallas_call(
        _ret_kernel,
        grid=(B, RET_HEADS),
        in_specs=[
            blk(CB_RQ), blk(CB_RK), blk(CB_RV), blk(CB_RG),
            pl.BlockSpec((S, LANES), lambda b, h: (0, 0)),
            pl.BlockSpec((S, LANES), lambda b, h: (0, 0)),
            pl.BlockSpec((None, 1, LANES), lambda b, h: (h, 0, 0)),
            pl.BlockSpec((None, 1, LANES), lambda b, h: (l, 0, h)),
        ],
        out_specs=pl.BlockSpec((S, LANES), lambda b, h: (b, h)),
        out_shape=jax.ShapeDtypeStruct((T, RET_HEADS * RET_DV), BF16),
        compiler_params=_cparams("parallel", "parallel"),
        name="retention",
    )(z, z, z, z, cos, sin, lg, gn)


def _softcap(x):
    return GATE_SOFTCAP * jnp.tanh(x * (1.0 / GATE_SOFTCAP))


def _log_sigmoid(x):
    return jnp.minimum(x, 0.0) - jnp.log(1.0 + jnp.exp(-jnp.abs(x)))


def _cumsum_pow2(x, axis):
    n = x.shape[axis]
    idx = lax.broadcasted_iota(jnp.int32, x.shape, axis)
    s = 1
    while s < n:
        x = x + jnp.where(idx >= s, pltpu.roll(x, s, axis), 0.0)
        s *= 2
    return x


def _mlstm_kernel(qk_ref, v_ref, op_ref, ir_ref, fr_ref, ic_ref, fc_ref, bi_ref, bf_ref, nw_ref,
                  o_ref, brow_ref):
    S = qk_ref.shape[0]
    C = ML_CHUNK
    N = S // C
    bi = bi_ref[...]
    bfg = bf_ref[...]
    i_r = _softcap(ir_ref[...] + bi)
    f_r = _log_sigmoid(_softcap(fr_ref[...] + bfg))
    i_c = _softcap(ic_ref[...] + bi[:, :N])
    f_c = _log_sigmoid(_softcap(fc_ref[...] + bfg[:, :N]))
    a_r = _cumsum_pow2(f_r, 1)
    a_c = _cumsum_pow2(f_c, 0)
    brow_ref[...] = i_r - a_r
    a_last = a_c[C - 1:C, :]
    w_end = a_last - a_c + i_c
    g_loc = jnp.max(w_end, axis=0, keepdims=True)
    w_exp = jnp.exp(w_end - g_loc)

    lane_n = lax.broadcasted_iota(jnp.int32, (C, N), 1)
    lane_1 = lax.broadcasted_iota(jnp.int32, (1, N), 1)
    ii = lax.broadcasted_iota(jnp.int32, (C, C), 0)
    jj = lax.broadcasted_iota(jnp.int32, (C, C), 1)
    causal = ii >= jj
    ones_blk = jnp.where(lax.broadcasted_iota(jnp.int32, (C, LANES), 1) == 0, 1.0, 0.0).astype(BF16)
    nw = nw_ref[...]

    def body(n, carry):
        state, m_st = carry
        r = pl.ds(pl.multiple_of(n * C, C), C)
        pick = lane_n == n
        a_col = jnp.sum(jnp.where(pick, a_c, 0.0), axis=1, keepdims=True)
        wx_col = jnp.sum(jnp.where(pick, w_exp, 0.0), axis=1, keepdims=True)
        pick1 = lane_1 == n
        a_l = jnp.sum(jnp.where(pick1, a_last, 0.0), axis=1, keepdims=True)
        g_l = jnp.sum(jnp.where(pick1, g_loc, 0.0), axis=1, keepdims=True)
        b_row = brow_ref[pl.ds(n, 1), :]

        qk = qk_ref[r, :]
        q = qk[:, :ML_DK]
        k = qk[:, ML_DK:]
        v_aug = jnp.concatenate([v_ref[r, :], ones_blk], axis=1)
        s = lax.dot_general(q, k, (((1,), (1,)), ((), ())), preferred_element_type=F32) * (ML_DK ** -0.5)
        d_log = jnp.where(causal, a_col + b_row, NEG_BIG)
        inter_log = a_col + m_st
        m_row = jnp.maximum(jnp.max(d_log, axis=1, keepdims=True), inter_log)
        qkw = s * jnp.exp(d_log - m_row)
        s_inter = jnp.exp(inter_log - m_row)
        nd = (jnp.dot(qkw.astype(BF16), v_aug, preferred_element_type=F32)
              + jnp.dot(q, state.astype(BF16), preferred_element_type=F32) * s_inter)
        num = nd[:, :ML_DV]
        den = nd[:, ML_DV:ML_DV + 1]
        den = jnp.maximum(jnp.abs(den), jnp.exp(-m_row))
        hh = num / den
        hh = _rms(hh, nw, EPS)
        o_ref[r, :] = (_sigmoid(op_ref[r, :].astype(F32)) * hh).astype(BF16)

        kw = (k.astype(F32) * (wx_col * (ML_DK ** -0.5))).astype(BF16)
        kv = lax.dot_general(kw, v_aug, (((0,), (0,)), ((), ())), preferred_element_type=F32)
        m_new = jnp.maximum(a_l + m_st, g_l)
        s_old = jnp.exp(a_l + m_st - m_new)
        s_new = jnp.exp(g_l - m_new)
        return s_old * state + s_new * kv, m_new

    lax.fori_loop(0, N, body, (jnp.zeros((ML_DK, 2 * LANES), F32), jnp.zeros((1, 1), F32)))


def _mlstm(z, g_rows, g_cols, bias, nw, l, B, S):
    T = B * S
    N = S // ML_CHUNK
    H = ML_HEADS
    blk = lambda cb: pl.BlockSpec((S, LANES), lambda b, h: (b, cb + h))
    return pl.pallas_call(
        _mlstm_kernel,
        grid=(B, H),
        in_specs=[
            blk(CB_MQK), blk(CB_MV), blk(CB_MO),
            pl.BlockSpec((None, None, N, ML_CHUNK), lambda b, h: (b, h, 0, 0)),
            pl.BlockSpec((None, None, N, ML_CHUNK), lambda b, h: (b, H + h, 0, 0)),
            pl.BlockSpec((None, None, ML_CHUNK, N), lambda b, h: (b, h, 0, 0)),
            pl.BlockSpec((None, None, ML_CHUNK, N), lambda b, h: (b, H + h, 0, 0)),
            pl.BlockSpec((None, 1, LANES), lambda b, h: (h, 0, 0)),
            pl.BlockSpec((None, 1, LANES), lambda b, h: (H + h, 0, 0)),
            pl.BlockSpec((None, 1, LANES), lambda b, h: (l, 0, h)),
        ],
        out_specs=pl.BlockSpec((S, LANES), lambda b, h: (b, h)),
        out_shape=jax.ShapeDtypeStruct((T, H * ML_DV), BF16),
        scratch_shapes=[pltpu.VMEM((N, ML_CHUNK), F32)],
        compiler_params=_cparams("parallel", "parallel"),
        name="mlstm",
    )(z, z, z, g_rows, g_rows, g_cols, g_cols, bias, bias, nw)


def _moba_kernel(q_ref, k_ref, v_ref, cos_ref, sin_ref, qw_ref, kw_ref, o_ref, kn_ref, kmean_ref):
    S = k_ref.shape[0]
    BLK = MB_BLOCK
    NB = S // BLK
    qi = pl.program_id(2)

    @pl.when(qi == 0)
    def _():
        kw = kw_ref[...]
        for jb in range(NB):
            r = pl.ds(jb * BLK, BLK)
            kn = _rot(_rms(k_ref[r, :].astype(F32), kw, EPS), cos_ref[r, :], sin_ref[r, :])
            kn_ref[r, :] = kn.astype(BF16)
            kmean_ref[jb:jb + 1, :] = jnp.mean(kn, axis=0, keepdims=True)

    rq = pl.ds(pl.multiple_of(qi * BLK, BLK), BLK)
    qn = _rot(_rms(q_ref[...].astype(F32), qw_ref[...], EPS), cos_ref[rq, :], sin_ref[rq, :])

    lane = lax.broadcasted_iota(jnp.int32, (BLK, LANES), 1)
    gcols = [jnp.sum(qn * kmean_ref[jb:jb + 1, :], axis=1, keepdims=True) for jb in range(NB)]
    gate = jnp.zeros((BLK, LANES), F32)
    for jb in range(NB):
        gate = jnp.where(lane == jb, gcols[jb], gate)
    past = lane < qi
    sel = jnp.zeros((BLK, LANES), F32)
    for jb in range(NB):
        gj = gcols[jb]
        beats = past & ((gate > gj) | ((gate == gj) & (lane < jb)))
        rank = jnp.sum(jnp.where(beats, 1.0, 0.0), axis=1, keepdims=True)
        sel = jnp.where((lane == jb) & (rank < float(MB_TOPK)), 1.0, sel)

    qs = (qn * (MB_DH ** -0.5)).astype(BF16)
    nt = (((1,), (1,)), ((), ()))
    s = lax.dot_general(qs, kn_ref[rq, :], nt, preferred_element_type=F32)
    row = lax.broadcasted_iota(jnp.int32, (BLK, BLK), 0)
    col = lax.broadcasted_iota(jnp.int32, (BLK, BLK), 1)
    s = jnp.where(col <= row, s, NEG_BIG)
    m0 = jnp.max(s, axis=1, keepdims=True)
    p = jnp.exp(s - m0)
    l0 = jnp.sum(p, axis=1, keepdims=True)
    acc0 = jnp.dot(p.astype(BF16), v_ref[rq, :], preferred_element_type=F32)

    def body(j, carry):
        m, l, acc = carry
        r = pl.ds(pl.multiple_of(j * BLK, BLK), BLK)
        chosen = jnp.sum(jnp.where(lane == j, sel, 0.0), axis=1, keepdims=True) > 0.5
        sj = lax.dot_general(qs, kn_ref[r, :], nt, preferred_element_type=F32)
        sj = jnp.where(chosen, sj, NEG_BIG)
        m_new = jnp.maximum(m, jnp.max(sj, axis=1, keepdims=True))
        alpha = jnp.exp(m - m_new)
        pj = jnp.exp(sj - m_new)
        l_new = alpha * l + jnp.sum(pj, axis=1, keepdims=True)
        acc_new = alpha * acc + jnp.dot(pj.astype(BF16), v_ref[r, :], preferred_element_type=F32)
        return m_new, l_new, acc_new

    m, l, acc = lax.fori_loop(0, qi, body, (m0, l0, acc0))
    o_ref[...] = (acc / l).astype(BF16)


def _moba(z, cos, sin, qw, kw, l, B, S):
    T = B * S
    NB = S // MB_BLOCK
    H = MB_HEADS
    return pl.pallas_call(
        _moba_kernel,
        grid=(B, H, NB),
        in_specs=[
            pl.BlockSpec((MB_BLOCK, LANES), lambda b, h, i: (b * NB + i, CB_AQ + h)),
            pl.BlockSpec((S, LANES), lambda b, h, i: (b, CB_AK + h)),
            pl.BlockSpec((S, LANES), lambda b, h, i: (b, CB_AV + h)),
            pl.BlockSpec((S, LANES), lambda b, h, i: (0, 0)),
            pl.BlockSpec((S, LANES), lambda b, h, i: (0, 0)),
            pl.BlockSpec((None, 1, LANES), lambda b, h, i: (l, 0, 0)),
            pl.BlockSpec((None, 1, LANES), lambda b, h, i: (l, 0, 0)),
        ],
        out_specs=pl.BlockSpec((MB_BLOCK, LANES), lambda b, h, i: (b * NB + i, h)),
        out_shape=jax.ShapeDtypeStruct((T, H * MB_DH), BF16),
        scratch_shapes=[pltpu.VMEM((S, LANES), BF16), pltpu.VMEM((NB, LANES), F32)],
        compiler_params=_cparams("parallel", "parallel", "arbitrary"),
        name="moba",
    )(z, z, z, cos, sin, qw, kw)


def _merge_kernel(ya_ref, yb_ref, yc_ref, ga_ref, gb_ref, gc_ref, wa_ref, wb_ref, wc_ref, wo_ref,
                  x_ref, o_ref, acc_ref):
    c = pl.program_id(1)

    def branch(y_ref, w_ref, g_ref):
        p = jnp.dot(y_ref[...], w_ref[...], preferred_element_type=F32)
        return _sigmoid(g_ref[...].astype(F32)) * p

    m = branch(ya_ref, wa_ref, ga_ref) + branch(yb_ref, wb_ref, gb_ref) + branch(yc_ref, wc_ref, gc_ref)
    contrib = jnp.dot(m.astype(BF16), wo_ref[...], preferred_element_type=F32)

    @pl.when(c == 0)
    def _():
        acc_ref[...] = contrib

    @pl.when(c > 0)
    def _():
        acc_ref[...] += contrib

    @pl.when(c == pl.num_programs(1) - 1)
    def _():
        o_ref[...] = x_ref[...] + acc_ref[...]


def _merge(ya, yb, yc, z, wpa, wpb, wpc, wout, x2, l, tm, tc):
    T, D = x2.shape
    W = ya.shape[1]
    yblk = pl.BlockSpec((tm, W), lambda i, c: (i, 0))
    gblk = lambda col: pl.BlockSpec((tm, tc), lambda i, c: (i, col // tc + c))
    wblk = pl.BlockSpec((None, W, tc), lambda i, c: (l, 0, c))
    return pl.pallas_call(
        _merge_kernel,
        grid=(T // tm, D // tc),
        in_specs=[
            yblk, yblk, yblk, gblk(COL_GA), gblk(COL_GB), gblk(COL_GC), wblk, wblk, wblk,
            pl.BlockSpec((None, tc, D), lambda i, c: (l, c, 0)),
            pl.BlockSpec((tm, D), lambda i, c: (i, 0)),
        ],
        out_specs=pl.BlockSpec((tm, D), lambda i, c: (i, 0)),
        out_shape=jax.ShapeDtypeStruct((T, D), F32),
        scratch_shapes=[pltpu.VMEM((tm, D), F32)],
        compiler_params=_cparams("parallel", "arbitrary"),
        name="merge",
    )(ya, yb, yc, z, z, z, wpa, wpb, wpc, wout, x2)


def _ffn_kernel(x_ref, g_ref, wa_ref, wb_ref, cwa_ref, cwb_ref, cba_ref, cbb_ref, wd_ref,
                o_ref, h_ref, acc_ref, ta_ref, tb_ref, *, tiles_per_seq):
    i = pl.program_id(0)
    c = pl.program_id(1)
    tm = x_ref.shape[0]
    tc = wa_ref.shape[1]

    @pl.when(c == 0)
    def _():
        h_ref[...] = _rms(x_ref[...], g_ref[...], EPS).astype(BF16)

    h = h_ref[...]
    seq_start = (i % tiles_per_seq) == 0
    row = lax.broadcasted_iota(jnp.int32, (tm, tc), 0)

    def conv(w_ref, cw_ref, cb_ref, t_ref):
        u = jnp.dot(h, w_ref[...], preferred_element_type=F32)
        @pl.when(seq_start)
        def _():
            t_ref[c] = jnp.zeros((8, tc), F32)

        prev = t_ref[c]
        t_ref[c] = u[tm - 8:, :]
        p1 = prev[7:8, :]
        p2 = prev[6:7, :]
        u1 = jnp.where(row == 0, p1, pltpu.roll(u, 1, 0))
        u2 = jnp.where(row == 0, p2, jnp.where(row == 1, p1, pltpu.roll(u, 2, 0)))
        cw = cw_ref[...]
        return cb_ref[...] + u2 * cw[0:1, :] + u1 * cw[1:2, :] + u * cw[2:3, :]

    a = conv(wa_ref, cwa_ref, cba_ref, ta_ref)
    b = conv(wb_ref, cwb_ref, cbb_ref, tb_ref)
    act = (a * _sigmoid(a) * b).astype(BF16)
    contrib = jnp.dot(act, wd_ref[...], preferred_element_type=F32)

    @pl.when(c == 0)
    def _():
        acc_ref[...] = contrib

    @pl.when(c > 0)
    def _():
        acc_ref[...] += contrib

    @pl.when(c == pl.num_programs(1) - 1)
    def _():
        o_ref[...] = x_ref[...] + acc_ref[...]


def _ffn(x2, norm_w, w_up, conv_w, conv_b, w_down, l, S, tm, tc):
    T, D = x2.shape
    F = w_down.shape[1]
    nc = F // tc
    wup = lambda off: pl.BlockSpec((None, D, tc), lambda i, c: (l, 0, off + c))
    cw = lambda off: pl.BlockSpec((None, FFN_CONV, tc), lambda i, c: (l, 0, off + c))
    cb = lambda off: pl.BlockSpec((None, 1, tc), lambda i, c: (l, 0, off + c))
    return pl.pallas_call(
        functools.partial(_ffn_kernel, tiles_per_seq=S // tm),
        grid=(T // tm, nc),
        in_specs=[
            pl.BlockSpec((tm, D), lambda i, c: (i, 0)),
            pl.BlockSpec((None, 1, D), lambda i, c: (l, 0, 0)),
            wup(0), wup(nc), cw(0), cw(nc), cb(0), cb(nc),
            pl.BlockSpec((None, tc, D), lambda i, c: (l, c, 0)),
        ],
        out_specs=pl.BlockSpec((tm, D), lambda i, c: (i, 0)),
        out_shape=jax.ShapeDtypeStruct((T, D), F32),
        scratch_shapes=[
            pltpu.VMEM((tm, D), BF16),
            pltpu.VMEM((tm, D), F32),
            pltpu.VMEM((nc, 8, tc), F32),
            pltpu.VMEM((nc, 8, tc), F32),
        ],
        compiler_params=_cparams("arbitrary", "arbitrary"),
        name="conv_glu_ffn",
    )(x2, norm_w, w_up, w_up, conv_w, conv_w, conv_b, conv_b, w_down)


def _rot_tables(S, inv_freq):
    ang = jnp.arange(S, dtype=F32)[:, None] * inv_freq[None, :]
    cos = jnp.cos(ang)
    sin = jnp.sin(ang)
    return jnp.concatenate([cos, cos], axis=1), jnp.concatenate([-sin, sin], axis=1)


def _permute_w_in(w_in):
    L, D, _ = w_in.shape
    mq = w_in[:, :, 4096:4608].reshape(L, D, ML_HEADS, ML_DK)
    mk = w_in[:, :, 4608:5120].reshape(L, D, ML_HEADS, ML_DK)
    mqk = jnp.concatenate([mq, mk], axis=3).reshape(L, D, 2 * ML_HEADS * ML_DK)
    w_main = jnp.concatenate([w_in[:, :, :4096], mqk, w_in[:, :, 5120:7168], w_in[:, :, 7184:]], axis=2)
    w_gate = jnp.pad(w_in[:, :, 7168:7184], ((0, 0), (0, 0), (0, LANES - 2 * ML_HEADS)))
    return w_main.astype(BF16), w_gate.astype(BF16)


TM_PROJ, TN_PROJ = 1024, 1024
TM_MERGE, TC_MERGE = 512, 512
TM_FFN, TC_FFN = 512, 512


def kernel(x, w_in, b_ig, b_fg, norm_mix, ret_gn, ml_norm, q_norm, k_norm, w_pa, w_pb, w_pc, w_out,
           norm_ffn, w_up, conv_w, conv_b, w_down):
    B, S, D = x.shape
    T = B * S
    L = w_in.shape[0]

    w_main, w_gate = _permute_w_in(w_in)
    wpa, wpb, wpc, wout = (w.astype(BF16) for w in (w_pa, w_pb, w_pc, w_out))
    wup = w_up.astype(BF16)
    wdown = w_down.astype(BF16)
    norm_mix3 = norm_mix.reshape(L, 1, D)
    norm_ffn3 = norm_ffn.reshape(L, 1, D)
    ret_gn3 = ret_gn.reshape(L, 1, -1)
    ml_norm3 = ml_norm.reshape(L, 1, -1)
    q_norm3 = q_norm.reshape(L, 1, MB_DH)
    k_norm3 = k_norm.reshape(L, 1, MB_DH)
    conv_b3 = conv_b.reshape(L, 1, -1)

    ret_freq = 1.0 / (ROPE_THETA ** jnp.linspace(0.0, 1.0, RET_DK // 2, dtype=F32))
    rope_freq = 1.0 / (ROPE_THETA ** (jnp.arange(0, MB_DH, 2, dtype=F32) / MB_DH))
    ret_cos, ret_sin = _rot_tables(S, ret_freq)
    mb_cos, mb_sin = _rot_tables(S, rope_freq)
    log_gamma = jnp.log1p(-jnp.exp2(-5.0 - jnp.arange(RET_HEADS, dtype=F32)))
    lg = jnp.broadcast_to(log_gamma[:, None, None], (RET_HEADS, 1, LANES))

    N = S // ML_CHUNK
    x2 = x.reshape(T, D)
    for l in range(L):
        z, zg = _in_proj(x2, norm_mix3, w_main, w_gate, l, min(TM_PROJ, S), TN_PROJ)
        gates = zg[:, :2 * ML_HEADS].reshape(B, N, ML_CHUNK, 2 * ML_HEADS)
        g_rows = gates.transpose(0, 3, 1, 2)
        g_cols = gates.transpose(0, 3, 2, 1)
        bias = jnp.concatenate([b_ig[l], b_fg[l]])
        bias = jnp.broadcast_to(bias[:, None, None], (2 * ML_HEADS, 1, LANES))
        y_a = _retention(z, ret_cos, ret_sin, lg, ret_gn3, l, B, S)
        y_b = _mlstm(z, g_rows, g_cols, bias, ml_norm3, l, B, S)
        y_c = _moba(z, mb_cos, mb_sin, q_norm3, k_norm3, l, B, S)
        x2 = _merge(y_a, y_b, y_c, z, wpa, wpb, wpc, wout, x2, l, min(TM_MERGE, S), TC_MERGE)
        x2 = _ffn(x2, norm_ffn3, wup, conv_w, conv_b3, wdown, l, S, min(TM_FFN, S), TC_FFN)
    return x2.reshape(B, S, D)
```

```python
import functools

import jax
import jax.numpy as jnp
from jax import lax
from jax.experimental import pallas as pl
from jax.experimental.pallas import tpu as pltpu

F32 = jnp.float32
BF16 = jnp.bfloat16

D_MODEL = 2048
DEPTH = 4
RET_HEADS = 8
RET_DK = 128
RET_DV = 128
RET_CHUNK = 128
GN_EPS = 1e-5
ML_HEADS = 8
ML_DK = 64
ML_DV = 128
ML_CHUNK = 128
GATE_SOFTCAP = 15.0
MB_HEADS = 8
MB_DH = 128
MB_BLOCK = 256
MB_TOPK = 3
ROPE_THETA = 10000.0
D_FF = 5632
FFN_CONV = 3
EPS = 1e-6

LANES = 128
N_MAIN = 16384
CB_RQ, CB_RK, CB_RV, CB_RG = 0, 8, 16, 24
CB_MQK, CB_MV, CB_MO = 32, 40, 48
CB_AQ, CB_AK, CB_AV = 56, 64, 72
COL_GA, COL_GB, COL_GC = 10240, 12288, 14336
NEG_BIG = -1e30
LOG2E = 1.4426950408889634

VMEM_LIMIT_BYTES = 56 * 1024 * 1024


def _cparams(*sem):
    return pltpu.CompilerParams(dimension_semantics=sem, vmem_limit_bytes=VMEM_LIMIT_BYTES)


def _rms(x, g, eps):
    return x * lax.rsqrt(jnp.mean(x * x, axis=-1, keepdims=True) + eps) * g


def _sigmoid(x):
    return 1.0 / (1.0 + jnp.exp(-x))


def _rot(x, cos, sin_signed):
    return x * cos + pltpu.roll(x, LANES // 2, 1) * sin_signed


def _inproj_kernel(x_ref, g_ref, w_ref, wg_ref, z_ref, zg_ref, h_ref):
    @pl.when(pl.program_id(1) == 0)
    def _():
        hb = _rms(x_ref[...], g_ref[...], EPS).astype(BF16)
        h_ref[...] = hb
        zg_ref[...] = jnp.dot(hb, wg_ref[...], preferred_element_type=F32)

    z_ref[...] = jnp.dot(h_ref[...], w_ref[...], preferred_element_type=F32).astype(BF16)


def _in_proj(x2, norm_w, w_main, w_gate, l, tm, tn):
    T, D = x2.shape
    N = w_main.shape[-1]
    return pl.pallas_call(
        _inproj_kernel,
        grid=(T // tm, N // tn),
        in_specs=[
            pl.BlockSpec((tm, D), lambda i, j: (i, 0)),
            pl.BlockSpec((None, 1, D), lambda i, j: (l, 0, 0)),
            pl.BlockSpec((None, D, tn), lambda i, j: (l, 0, j)),
            pl.BlockSpec((None, D, LANES), lambda i, j: (l, 0, 0)),
        ],
        out_specs=[
            pl.BlockSpec((tm, tn), lambda i, j: (i, j)),
            pl.BlockSpec((tm, LANES), lambda i, j: (i, 0)),
        ],
        out_shape=[jax.ShapeDtypeStruct((T, N), BF16), jax.ShapeDtypeStruct((T, LANES), F32)],
        scratch_shapes=[pltpu.VMEM((tm, D), BF16)],
        compiler_params=_cparams("parallel", "arbitrary"),
        name="in_proj",
    )(x2, norm_w, w_main, w_gate)


def _ret_kernel(q_ref, k_ref, v_ref, g_ref, cos_ref, sin_ref, lg_ref, gn_ref, o_ref):
    S = q_ref.shape[0]
    C = RET_CHUNK
    lg = lg_ref[...]
    ii = lax.broadcasted_iota(jnp.int32, (C, C), 0)
    jj = lax.broadcasted_iota(jnp.int32, (C, C), 1)
    decay = jnp.where(ii >= jj, jnp.exp(lg * jnp.maximum(ii - jj, 0).astype(F32)), 0.0)
    pos = lax.broadcasted_iota(jnp.int32, (C, LANES), 0).astype(F32)
    kdec = jnp.exp((C - 1.0 - pos) * lg)
    qdec = jnp.exp((pos + 1.0) * lg)
    cdec = jnp.exp(float(C) * lg)
    gn = gn_ref[...]

    state = jnp.zeros((RET_DK, RET_DV), F32)
    for n in range(S // C):
        r = pl.ds(n * C, C)
        cs = cos_ref[r, :]
        sn = sin_ref[r, :]
        q = _rot(q_ref[r, :].astype(F32), cs, sn)
        k = _rot(k_ref[r, :].astype(F32), cs, sn) * (RET_DK ** -0.5)
        v = v_ref[r, :]
        s = lax.dot_general(q.astype(BF16), k.astype(BF16), (((1,), (1,)), ((), ())),
                            preferred_element_type=F32) * decay
        intra = jnp.dot(s.astype(BF16), v, preferred_element_type=F32)
        inter = jnp.dot((q * qdec).astype(BF16), state.astype(BF16), preferred_element_type=F32)
        kv = lax.dot_general((k * kdec).astype(BF16), v, (((0,), (0,)), ((), ())),
                             preferred_element_type=F32)
        o = intra + inter
        oc = o - jnp.mean(o, axis=-1, keepdims=True)
        on = oc * lax.rsqrt(jnp.mean(oc * oc, axis=-1, keepdims=True) + GN_EPS) * gn
        g = g_ref[r, :].astype(F32)
        o_ref[r, :] = (g * _sigmoid(g) * on).astype(BF16)
        state = state * cdec + kv


def _retention(z, cos, sin, lg, gn, l, B, S):
    T = B * S
    blk = lambda cb: pl.BlockSpec((S, LANES), lambda b, h: (b, cb + h))
    return pl.pallas_call(
        _ret_kernel,
        grid=(B, RET_HEADS),
        in_specs=[
            blk(CB_RQ), blk(CB_RK), blk(CB_RV), blk(CB_RG),
            pl.BlockSpec((S, LANES), lambda b, h: (0, 0)),
            pl.BlockSpec((S, LANES), lambda b, h: (0, 0)),
            pl.BlockSpec((None, 1, LANES), lambda b, h: (h, 0, 0)),
            pl.BlockSpec((None, 1, LANES), lambda b, h: (l, 0, h)),
        ],
        out_specs=pl.BlockSpec((S, LANES), lambda b, h: (b, h)),
        out_shape=jax.ShapeDtypeStruct((T, RET_HEADS * RET_DV), BF16),
        compiler_params=_cparams("parallel", "parallel"),
        name="retention",
    )(z, z, z, z, cos, sin, lg, gn)


def _softcap(x):
    return GATE_SOFTCAP * jnp.tanh(x * (1.0 / GATE_SOFTCAP))


def _log_sigmoid(x):
    return jnp.minimum(x, 0.0) - jnp.log(1.0 + jnp.exp(-jnp.abs(x)))


def _cumsum_pow2(x, axis):
    n = x.shape[axis]
    idx = lax.broadcasted_iota(jnp.int32, x.shape, axis)
    s = 1
    while s < n:
        x = x + jnp.where(idx >= s, pltpu.roll(x, s, axis), 0.0)
        s *= 2
    return x


def _cummax_pow2(x, axis):
    n = x.shape[axis]
    idx = lax.broadcasted_iota(jnp.int32, x.shape, axis)
    s = 1
    while s < n:
        x = jnp.maximum(x, jnp.where(idx >= s, pltpu.roll(x, s, axis), NEG_BIG))
        s *= 2
    return x


def _mlstm_kernel(qk_ref, v_ref, op_ref, ir_ref, fr_ref, ic_ref, fc_ref, bi_ref, bf_ref, nw_ref, o_ref,
                  kv_ref, st_ref):
    S = qk_ref.shape[0]
    C = ML_CHUNK
    N = S // C
    bi = bi_ref[...]
    bfg = bf_ref[...]
    i_r = _softcap(ir_ref[...] + bi)
    f_r = _log_sigmoid(_softcap(fr_ref[...] + bfg))
    i_c = _softcap(ic_ref[...] + bi[:, :N])
    f_c = _log_sigmoid(_softcap(fc_ref[...] + bfg[:, :N]))
    a_c = _cumsum_pow2(f_c, 0)
    b_r = i_r - _cumsum_pow2(f_r, 1)
    b_c = i_c - a_c
    a_last = a_c[C - 1:C, :]
    w_end = a_last + b_c
    g_loc = jnp.max(w_end, axis=0, keepdims=True)
    w_exp = jnp.exp(w_end - g_loc) * (ML_DK ** -0.5)

    lane_n = lax.broadcasted_iota(jnp.int32, (1, N), 1)
    m_st = jnp.zeros((1, 1), F32)
    m_prev = jnp.zeros((1, N), F32)
    for n in range(N):
        m_prev = jnp.where(lane_n == n, m_st, m_prev)
        m_st = jnp.maximum(a_last[:, n:n + 1] + m_st, g_loc[:, n:n + 1])
    m_next = jnp.maximum(a_last + m_prev, g_loc)
    s_old = jnp.exp(a_last + m_prev - m_next)
    s_new = jnp.exp(g_loc - m_next)
    m_all = jnp.maximum(_cummax_pow2(b_c, 0), m_prev)
    s_inter = jnp.exp(m_prev - m_all)
    exp_neg_m_row = jnp.exp(-(a_c + m_all))

    ones_blk = jnp.where(lax.broadcasted_iota(jnp.int32, (C, LANES), 1) == 0, 1.0, 0.0).astype(BF16)

    for n in range(N):
        r = pl.ds(n * C, C)
        v_aug = jnp.concatenate([v_ref[r, :], ones_blk], axis=1)
        kw = (qk_ref[r, :][:, ML_DK:].astype(F32) * w_exp[:, n:n + 1]).astype(BF16)
        kv_ref[n] = lax.dot_general(kw, v_aug, (((0,), (0,)), ((), ())), preferred_element_type=F32)

    state = jnp.zeros((ML_DK, 2 * LANES), F32)
    for n in range(N):
        st_ref[n] = state.astype(BF16)
        state = s_old[:, n:n + 1] * state + s_new[:, n:n + 1] * kv_ref[n]

    ii = lax.broadcasted_iota(jnp.int32, (C, C), 0)
    jj = lax.broadcasted_iota(jnp.int32, (C, C), 1)
    causal = ii >= jj
    nw = nw_ref[...]
    for n in range(N):
        r = pl.ds(n * C, C)
        qk = qk_ref[r, :]
        q = qk[:, :ML_DK]
        k = qk[:, ML_DK:]
        v_aug = jnp.concatenate([v_ref[r, :], ones_blk], axis=1)
        s = lax.dot_general(q, k, (((1,), (1,)), ((), ())), preferred_element_type=F32) * (ML_DK ** -0.5)
        qkw = s * jnp.where(causal, jnp.exp(b_r[n:n + 1, :] - m_all[:, n:n + 1]), 0.0)
        nd = (jnp.dot(qkw.astype(BF16), v_aug, preferred_element_type=F32)
              + jnp.dot(q, st_ref[n], preferred_element_type=F32) * s_inter[:, n:n + 1])
        num = nd[:, :ML_DV]
        den = jnp.maximum(jnp.abs(nd[:, ML_DV:ML_DV + 1]), exp_neg_m_row[:, n:n + 1])
        hh = _rms(num / den, nw, EPS)
        o_ref[r, :] = (_sigmoid(op_ref[r, :].astype(F32)) * hh).astype(BF16)


def _mlstm(z, g_rows, g_cols, bias, nw, l, B, S):
    T = B * S
    N = S // ML_CHUNK
    H = ML_HEADS
    blk = lambda cb: pl.BlockSpec((S, LANES), lambda b, h: (b, cb + h))
    return pl.pallas_call(
        _mlstm_kernel,
        grid=(B, H),
        in_specs=[
            blk(CB_MQK), blk(CB_MV), blk(CB_MO),
            pl.BlockSpec((None, None, N, ML_CHUNK), lambda b, h: (b, h, 0, 0)),
            pl.BlockSpec((None, None, N, ML_CHUNK), lambda b, h: (b, H + h, 0, 0)),
            pl.BlockSpec((None, None, ML_CHUNK, N), lambda b, h: (b, h, 0, 0)),
            pl.BlockSpec((None, None, ML_CHUNK, N), lambda b, h: (b, H + h, 0, 0)),
            pl.BlockSpec((None, 1, LANES), lambda b, h: (h, 0, 0)),
            pl.BlockSpec((None, 1, LANES), lambda b, h: (H + h, 0, 0)),
            pl.BlockSpec((None, 1, LANES), lambda b, h: (l, 0, h)),
        ],
        out_specs=pl.BlockSpec((S, LANES), lambda b, h: (b, h)),
        out_shape=jax.ShapeDtypeStruct((T, H * ML_DV), BF16),
        scratch_shapes=[
            pltpu.VMEM((N, ML_DK, 2 * LANES), F32),
            pltpu.VMEM((N, ML_DK, 2 * LANES), BF16),
        ],
        compiler_params=_cparams("parallel", "parallel"),
        name="mlstm",
    )(z, z, z, g_rows, g_rows, g_cols, g_cols, bias, bias, nw)


def _moba_kernel(q_ref, k_ref, v_ref, cos_ref, sin_ref, qw_ref, kw_ref, o_ref, kn_ref, vt_ref):
    S = k_ref.shape[0]
    BLK = MB_BLOCK
    NB = S // BLK
    nt = (((1,), (1,)), ((), ()))
    qw = qw_ref[...]
    kw = kw_ref[...]

    kmeans = []
    for jb in range(NB):
        r = pl.ds(jb * BLK, BLK)
        kn = _rot(_rms(k_ref[r, :].astype(F32), kw, EPS), cos_ref[r, :], sin_ref[r, :])
        kn_ref[r, :] = kn.astype(BF16)
        kmeans.append(jnp.mean(kn, axis=0, keepdims=True))
        vt_ref[:, r] = v_ref[r, :].astype(F32).T.astype(BF16)
    kmean = jnp.concatenate(kmeans, axis=0)

    blk_id = lax.broadcasted_iota(jnp.int32, (NB, BLK), 0)
    kpos = lax.broadcasted_iota(jnp.int32, (BLK, BLK), 0)
    qpos = lax.broadcasted_iota(jnp.int32, (BLK, BLK), 1)
    causal = kpos <= qpos

    for qi in range(NB):
        rq = pl.ds(qi * BLK, BLK)
        qn = _rot(_rms(q_ref[rq, :].astype(F32), qw, EPS), cos_ref[rq, :], sin_ref[rq, :])
        qs = (qn * (MB_DH ** -0.5 * LOG2E)).astype(BF16)
        s_all = lax.dot_general(kn_ref[pl.ds(0, (qi + 1) * BLK), :], qs, nt, preferred_element_type=F32)
        s_own = jnp.where(causal, s_all[qi * BLK:, :], NEG_BIG)
        m = jnp.max(s_own, axis=0, keepdims=True)
        if qi > 0:
            gate = lax.dot_general(kmean, qn, nt, precision=lax.Precision.HIGHEST, preferred_element_type=F32)
            past = blk_id < qi
            chosen = []
            for jb in range(qi):
                gj = gate[jb:jb + 1, :]
                beats = past & ((gate > gj) | ((gate == gj) & (blk_id < jb)))
                rank = jnp.sum(jnp.where(beats, 1.0, 0.0), axis=0, keepdims=True)
                chosen.append(rank < float(MB_TOPK))
            for jb in range(qi):
                cmax = jnp.max(s_all[jb * BLK:(jb + 1) * BLK, :], axis=0, keepdims=True)
                m = jnp.maximum(m, jnp.where(chosen[jb], cmax, NEG_BIG))
        ps = []
        l = jnp.zeros((1, BLK), F32)
        for jb in range(qi):
            pj = jnp.exp2(s_all[jb * BLK:(jb + 1) * BLK, :] - jnp.where(chosen[jb], m, -NEG_BIG))
            l = l + jnp.sum(pj, axis=0, keepdims=True)
            ps.append(pj.astype(BF16))
        p_own = jnp.exp2(s_own - m)
        l = l + jnp.sum(p_own, axis=0, keepdims=True)
        ps.append(p_own.astype(BF16))
        p_all = jnp.concatenate(ps, axis=0) if qi > 0 else ps[0]
        acc = jnp.dot(vt_ref[:, pl.ds(0, (qi + 1) * BLK)], p_all, preferred_element_type=F32)
        o_ref[rq, :] = (acc / l).T.astype(BF16)


def _moba(z, cos, sin, qw, kw, l, B, S):
    T = B * S
    H = MB_HEADS
    blk = lambda cb: pl.BlockSpec((S, LANES), lambda b, h: (b, cb + h))
    return pl.pallas_call(
        _moba_kernel,
        grid=(B, H),
        in_specs=[
            blk(CB_AQ), blk(CB_AK), blk(CB_AV),
            pl.BlockSpec((S, LANES), lambda b, h: (0, 0)),
            pl.BlockSpec((S, LANES), lambda b, h: (0, 0)),
            pl.BlockSpec((None, 1, LANES), lambda b, h: (l, 0, 0)),
            pl.BlockSpec((None, 1, LANES), lambda b, h: (l, 0, 0)),
        ],
        out_specs=pl.BlockSpec((S, LANES), lambda b, h: (b, h)),
        out_shape=jax.ShapeDtypeStruct((T, H * MB_DH), BF16),
        scratch_shapes=[pltpu.VMEM((S, LANES), BF16), pltpu.VMEM((LANES, S), BF16)],
        compiler_params=_cparams("parallel", "parallel"),
        name="moba",
    )(z, z, z, cos, sin, qw, kw)


def _merge_kernel(ya_ref, yb_ref, yc_ref, ga_ref, gb_ref, gc_ref, wa_ref, wb_ref, wc_ref, wo_ref,
                  x_ref, o_ref, *, rb):
    tm = x_ref.shape[0]

    @pl.when(pl.program_id(1) == 0)
    def _():
        o_ref[...] = x_ref[...]

    def up(k):
        rows = pl.ds(k * rb, rb)
        return tuple(jnp.dot(y_ref[rows, :], w_ref[...], preferred_element_type=F32)
                     for y_ref, w_ref in ((ya_ref, wa_ref), (yb_ref, wb_ref), (yc_ref, wc_ref)))

    def finish(k, prods):
        rows = pl.ds(k * rb, rb)
        m = sum(_sigmoid(g_ref[rows, :].astype(F32)) * p
                for g_ref, p in zip((ga_ref, gb_ref, gc_ref), prods))
        o_ref[rows, :] += jnp.dot(m.astype(BF16), wo_ref[...], preferred_element_type=F32)

    pend = None
    for k in range(tm // rb):
        prods = up(k)
        if pend is not None:
            finish(*pend)
        pend = (k, prods)
    finish(*pend)


def _merge(ya, yb, yc, z, wpa, wpb, wpc, wout, x2, l, tm, tc):
    T, D = x2.shape
    W = ya.shape[1]
    yblk = pl.BlockSpec((tm, W), lambda i, c: (i, 0))
    gblk = lambda col: pl.BlockSpec((tm, tc), lambda i, c: (i, col // tc + c))
    wblk = pl.BlockSpec((None, W, tc), lambda i, c: (l, 0, c))
    return pl.pallas_call(
        functools.partial(_merge_kernel, rb=min(RB_MERGE, tm)),
        grid=(T // tm, D // tc),
        in_specs=[
            yblk, yblk, yblk, gblk(COL_GA), gblk(COL_GB), gblk(COL_GC), wblk, wblk, wblk,
            pl.BlockSpec((None, tc, D), lambda i, c: (l, c, 0)),
            pl.BlockSpec((tm, D), lambda i, c: (i, 0)),
        ],
        out_specs=pl.BlockSpec((tm, D), lambda i, c: (i, 0)),
        out_shape=jax.ShapeDtypeStruct((T, D), F32),
        compiler_params=_cparams("parallel", "arbitrary"),
        name="merge",
    )(ya, yb, yc, z, z, z, wpa, wpb, wpc, wout, x2)


def _ffn_kernel(x_ref, g_ref, wa_ref, wb_ref, cwa_ref, cwb_ref, cba_ref, cbb_ref, wd_ref,
                o_ref, h_ref, ta_ref, tb_ref, *, tiles_per_seq, rb):
    i = pl.program_id(0)
    c = pl.program_id(1)
    tm = x_ref.shape[0]
    tc = wa_ref.shape[1]

    @pl.when(c == 0)
    def _():
        x = x_ref[...]
        h_ref[...] = _rms(x, g_ref[...], EPS).astype(BF16)
        o_ref[...] = x

    @pl.when((i % tiles_per_seq) == 0)
    def _():
        ta_ref[c] = jnp.zeros((8, tc), F32)
        tb_ref[c] = jnp.zeros((8, tc), F32)

    row = lax.broadcasted_iota(jnp.int32, (rb, tc), 0)
    cwa = cwa_ref[...]
    cwb = cwb_ref[...]
    cba = cba_ref[...]
    cbb = cbb_ref[...]

    def up(k):
        hs = h_ref[pl.ds(k * rb, rb), :]
        return (jnp.dot(hs, wa_ref[...], preferred_element_type=F32),
                jnp.dot(hs, wb_ref[...], preferred_element_type=F32))

    def conv(u, prev, cw, cb):
        p1 = prev[7:8, :]
        p2 = prev[6:7, :]
        u1 = jnp.where(row == 0, p1, pltpu.roll(u, 1, 0))
        u2 = jnp.where(row == 0, p2, jnp.where(row == 1, p1, pltpu.roll(u, 2, 0)))
        return cb + u2 * cw[0:1, :] + u1 * cw[1:2, :] + u * cw[2:3, :]

    def finish(k, ua, ub, pa, pb):
        a = conv(ua, pa, cwa, cba)
        b = conv(ub, pb, cwb, cbb)
        act = (a * _sigmoid(a) * b).astype(BF16)
        rows = pl.ds(k * rb, rb)
        o_ref[rows, :] += jnp.dot(act, wd_ref[...], preferred_element_type=F32)

    pa, pb = ta_ref[c], tb_ref[c]
    pend = None
    for k in range(tm // rb):
        ua, ub = up(k)
        if pend is not None:
            finish(*pend)
        pend = (k, ua, ub, pa, pb)
        pa, pb = ua[rb - 8:, :], ub[rb - 8:, :]
    finish(*pend)
    ta_ref[c] = pa
    tb_ref[c] = pb


def _ffn(x2, norm_w, w_up, conv_w, conv_b, w_down, l, S, tm, tc):
    T, D = x2.shape
    F = w_down.shape[1]
    nc = F // tc
    wup = lambda off: pl.BlockSpec((None, D, tc), lambda i, c: (l, 0, off + c))
    cw = lambda off: pl.BlockSpec((None, FFN_CONV, tc), lambda i, c: (l, 0, off + c))
    cb = lambda off: pl.BlockSpec((None, 1, tc), lambda i, c: (l, 0, off + c))
    return pl.pallas_call(
        functools.partial(_ffn_kernel, tiles_per_seq=S // tm, rb=min(RB_FFN, tm)),
        grid=(T // tm, nc),
        in_specs=[
            pl.BlockSpec((tm, D), lambda i, c: (i, 0)),
            pl.BlockSpec((None, 1, D), lambda i, c: (l, 0, 0)),
            wup(0), wup(nc), cw(0), cw(nc), cb(0), cb(nc),
            pl.BlockSpec((None, tc, D), lambda i, c: (l, c, 0)),
        ],
        out_specs=pl.BlockSpec((tm, D), lambda i, c: (i, 0)),
        out_shape=jax.ShapeDtypeStruct((T, D), F32),
        scratch_shapes=[
            pltpu.VMEM((tm, D), BF16),
            pltpu.VMEM((nc, 8, tc), F32),
            pltpu.VMEM((nc, 8, tc), F32),
        ],
        compiler_params=_cparams("arbitrary", "arbitrary"),
        name="conv_glu_ffn",
    )(x2, norm_w, w_up, w_up, conv_w, conv_w, conv_b, conv_b, w_down)


def _rot_tables(S, inv_freq):
    ang = jnp.arange(S, dtype=F32)[:, None] * inv_freq[None, :]
    cos = jnp.cos(ang)
    sin = jnp.sin(ang)
    return jnp.concatenate([cos, cos], axis=1), jnp.concatenate([-sin, sin], axis=1)


def _permute_w_in(w_in):
    L, D, _ = w_in.shape
    mq = w_in[:, :, 4096:4608].reshape(L, D, ML_HEADS, ML_DK)
    mk = w_in[:, :, 4608:5120].reshape(L, D, ML_HEADS, ML_DK)
    mqk = jnp.concatenate([mq, mk], axis=3).reshape(L, D, 2 * ML_HEADS * ML_DK)
    w_main = jnp.concatenate([w_in[:, :, :4096], mqk, w_in[:, :, 5120:7168], w_in[:, :, 7184:]], axis=2)
    w_gate = jnp.pad(w_in[:, :, 7168:7184], ((0, 0), (0, 0), (0, LANES - 2 * ML_HEADS)))
    return w_main.astype(BF16), w_gate.astype(BF16)


TM_PROJ, TN_PROJ = 1024, 1024
TM_MERGE, TC_MERGE = 512, 512
RB_MERGE = 256
TM_FFN, TC_FFN = 512, 512
RB_FFN = 256


def kernel(x, w_in, b_ig, b_fg, norm_mix, ret_gn, ml_norm, q_norm, k_norm, w_pa, w_pb, w_pc, w_out,
           norm_ffn, w_up, conv_w, conv_b, w_down):
    B, S, D = x.shape
    T = B * S
    L = w_in.shape[0]

    w_main, w_gate = _permute_w_in(w_in)
    wpa, wpb, wpc, wout = (w.astype(BF16) for w in (w_pa, w_pb, w_pc, w_out))
    wup = w_up.astype(BF16)
    wdown = w_down.astype(BF16)
    norm_mix3 = norm_mix.reshape(L, 1, D)
    norm_ffn3 = norm_ffn.reshape(L, 1, D)
    ret_gn3 = ret_gn.reshape(L, 1, -1)
    ml_norm3 = ml_norm.reshape(L, 1, -1)
    q_norm3 = q_norm.reshape(L, 1, MB_DH)
    k_norm3 = k_norm.reshape(L, 1, MB_DH)
    conv_b3 = conv_b.reshape(L, 1, -1)

    ret_freq = 1.0 / (ROPE_THETA ** jnp.linspace(0.0, 1.0, RET_DK // 2, dtype=F32))
    rope_freq = 1.0 / (ROPE_THETA ** (jnp.arange(0, MB_DH, 2, dtype=F32) / MB_DH))
    ret_cos, ret_sin = _rot_tables(S, ret_freq)
    mb_cos, mb_sin = _rot_tables(S, rope_freq)
    log_gamma = jnp.log1p(-jnp.exp2(-5.0 - jnp.arange(RET_HEADS, dtype=F32)))
    lg = jnp.broadcast_to(log_gamma[:, None, None], (RET_HEADS, 1, LANES))

    N = S // ML_CHUNK
    x2 = x.reshape(T, D)
    for l in range(L):
        z, zg = _in_proj(x2, norm_mix3, w_main, w_gate, l, min(TM_PROJ, S), TN_PROJ)
        gates = zg[:, :2 * ML_HEADS].reshape(B, N, ML_CHUNK, 2 * ML_HEADS)
        g_rows = gates.transpose(0, 3, 1, 2)
        g_cols = gates.transpose(0, 3, 2, 1)
        bias = jnp.concatenate([b_ig[l], b_fg[l]])
        bias = jnp.broadcast_to(bias[:, None, None], (2 * ML_HEADS, 1, LANES))
        y_a = _retention(z, ret_cos, ret_sin, lg, ret_gn3, l, B, S)
        y_b = _mlstm(z, g_rows, g_cols, bias, ml_norm3, l, B, S)
        y_c = _moba(z, mb_cos, mb_sin, q_norm3, k_norm3, l, B, S)
        x2 = _merge(y_a, y_b, y_c, z, wpa, wpb, wpc, wout, x2, l, min(TM_MERGE, S), TC_MERGE)
        x2 = _ffn(x2, norm_ffn3, wup, conv_w, conv_b3, wdown, l, S, min(TM_FFN, S), TC_FFN)
    return x2.reshape(B, S, D)
```

```python
import functools

import jax
import jax.numpy as jnp
from jax import lax
from jax.experimental import pallas as pl
from jax.experimental.pallas import tpu as pltpu

F32 = jnp.float32
BF16 = jnp.bfloat16

D_MODEL = 2048
DEPTH = 4
RET_HEADS = 8
RET_DK = 128
RET_DV = 128
RET_CHUNK = 128
GN_EPS = 1e-5
ML_HEADS = 8
ML_DK = 64
ML_DV = 128
ML_CHUNK = 128
GATE_SOFTCAP = 15.0
MB_HEADS = 8
MB_DH = 128
MB_BLOCK = 256
MB_TOPK = 3
ROPE_THETA = 10000.0
D_FF = 5632
FFN_CONV = 3
EPS = 1e-6

LANES = 128
N_MAIN = 16384
CB_RQ, CB_RK, CB_RV, CB_RG = 0, 8, 16, 24
CB_MQ, CB_MK, CB_MV, CB_MO = 32, 36, 40, 48
CB_AQ, CB_AK, CB_AV = 56, 64, 72
COL_GA, COL_GB, COL_GC = 10240, 12288, 14336
NEG_BIG = -1e30
LOG2E = 1.4426950408889634

VMEM_LIMIT_BYTES = 56 * 1024 * 1024


def _cparams(*sem):
    return pltpu.CompilerParams(dimension_semantics=sem, vmem_limit_bytes=VMEM_LIMIT_BYTES)


def _rms(x, g, eps):
    return x * lax.rsqrt(jnp.mean(x * x, axis=-1, keepdims=True) + eps) * g


def _sigmoid(x):
    return 1.0 / (1.0 + jnp.exp(-x))


def _rot(x, cos, sin_signed):
    return x * cos + pltpu.roll(x, LANES // 2, 1) * sin_signed


def _inproj_kernel(x_ref, g_ref, w_ref, wg_ref, z_ref, zg_ref, h_ref):
    @pl.when(pl.program_id(1) == 0)
    def _():
        hb = _rms(x_ref[...], g_ref[...], EPS).astype(BF16)
        h_ref[...] = hb
        zg_ref[...] = jnp.dot(hb, wg_ref[...], preferred_element_type=F32)

    z_ref[...] = jnp.dot(h_ref[...], w_ref[...], preferred_element_type=F32).astype(BF16)


def _in_proj(x2, norm_w, w_main, w_gate, l, tm, tn):
    T, D = x2.shape
    N = w_main.shape[-1]
    return pl.pallas_call(
        _inproj_kernel,
        grid=(T // tm, N // tn),
        in_specs=[
            pl.BlockSpec((tm, D), lambda i, j: (i, 0)),
            pl.BlockSpec((None, 1, D), lambda i, j: (l, 0, 0)),
            pl.BlockSpec((None, D, tn), lambda i, j: (l, 0, j)),
            pl.BlockSpec((None, D, LANES), lambda i, j: (l, 0, 0)),
        ],
        out_specs=[
            pl.BlockSpec((tm, tn), lambda i, j: (i, j)),
            pl.BlockSpec((tm, LANES), lambda i, j: (i, 0)),
        ],
        out_shape=[jax.ShapeDtypeStruct((T, N), BF16), jax.ShapeDtypeStruct((T, LANES), F32)],
        scratch_shapes=[pltpu.VMEM((tm, D), BF16)],
        compiler_params=_cparams("parallel", "arbitrary"),
        name="in_proj",
    )(x2, norm_w, w_main, w_gate)


def _ret_kernel(q_ref, k_ref, v_ref, g_ref, cos_ref, sin_ref, lg_ref, gn_ref, o_ref):
    S = q_ref.shape[0]
    C = RET_CHUNK
    lg = lg_ref[...]
    ii = lax.broadcasted_iota(jnp.int32, (C, C), 0)
    jj = lax.broadcasted_iota(jnp.int32, (C, C), 1)
    decay = jnp.where(ii >= jj, jnp.exp(lg * jnp.maximum(ii - jj, 0).astype(F32)), 0.0)
    pos = lax.broadcasted_iota(jnp.int32, (C, LANES), 0).astype(F32)
    kdec = jnp.exp((C - 1.0 - pos) * lg)
    qdec = jnp.exp((pos + 1.0) * lg)
    cdec = jnp.exp(float(C) * lg)
    gn = gn_ref[...]

    state = jnp.zeros((RET_DK, RET_DV), F32)
    for n in range(S // C):
        r = pl.ds(n * C, C)
        cs = cos_ref[r, :]
        sn = sin_ref[r, :]
        q = _rot(q_ref[r, :].astype(F32), cs, sn)
        k = _rot(k_ref[r, :].astype(F32), cs, sn) * (RET_DK ** -0.5)
        v = v_ref[r, :]
        s = lax.dot_general(q.astype(BF16), k.astype(BF16), (((1,), (1,)), ((), ())),
                            preferred_element_type=F32) * decay
        intra = jnp.dot(s.astype(BF16), v, preferred_element_type=F32)
        inter = jnp.dot((q * qdec).astype(BF16), state.astype(BF16), preferred_element_type=F32)
        kv = lax.dot_general((k * kdec).astype(BF16), v, (((0,), (0,)), ((), ())),
                             preferred_element_type=F32)
        o = intra + inter
        oc = o - jnp.mean(o, axis=-1, keepdims=True)
        on = oc * lax.rsqrt(jnp.mean(oc * oc, axis=-1, keepdims=True) + GN_EPS) * gn
        g = g_ref[r, :].astype(F32)
        o_ref[r, :] = (g * _sigmoid(g) * on).astype(BF16)
        state = state * cdec + kv


def _retention(z, cos, sin, lg, gn, l, B, S):
    T = B * S
    blk = lambda cb: pl.BlockSpec((S, LANES), lambda b, h: (b, cb + h))
    return pl.pallas_call(
        _ret_kernel,
        grid=(B, RET_HEADS),
        in_specs=[
            blk(CB_RQ), blk(CB_RK), blk(CB_RV), blk(CB_RG),
            pl.BlockSpec((S, LANES), lambda b, h: (0, 0)),
            pl.BlockSpec((S, LANES), lambda b, h: (0, 0)),
            pl.BlockSpec((None, 1, LANES), lambda b, h: (h, 0, 0)),
            pl.BlockSpec((None, 1, LANES), lambda b, h: (l, 0, h)),
        ],
        out_specs=pl.BlockSpec((S, LANES), lambda b, h: (b, h)),
        out_shape=jax.ShapeDtypeStruct((T, RET_HEADS * RET_DV), BF16),
        compiler_params=_cparams("parallel", "parallel"),
        name="retention",
    )(z, z, z, z, cos, sin, lg, gn)


def _softcap(x):
    return GATE_SOFTCAP * jnp.tanh(x * (1.0 / GATE_SOFTCAP))


def _log_sigmoid(x):
    return jnp.minimum(x, 0.0) - jnp.log(1.0 + jnp.exp(-jnp.abs(x)))


def _cumsum_pow2(x, axis):
    n = x.shape[axis]
    idx = lax.broadcasted_iota(jnp.int32, x.shape, axis)
    s = 1
    while s < n:
        x = x + jnp.where(idx >= s, pltpu.roll(x, s, axis), 0.0)
        s *= 2
    return x


def _cummax_pow2(x, axis):
    n = x.shape[axis]
    idx = lax.broadcasted_iota(jnp.int32, x.shape, axis)
    s = 1
    while s < n:
        x = jnp.maximum(x, jnp.where(idx >= s, pltpu.roll(x, s, axis), NEG_BIG))
        s *= 2
    return x


def _mlstm_kernel(q_ref, k_ref, v_ref, op_ref, ir_ref, fr_ref, ic_ref, fc_ref, bi_ref, bf_ref, nw_ref, o_ref,
                  kv_ref, st_ref):
    S = q_ref.shape[0]
    C = ML_CHUNK
    N = S // C
    DK = ML_DK
    lane = lax.broadcasted_iota(jnp.int32, (1, LANES), 1)
    lane_n = lax.broadcasted_iota(jnp.int32, (1, N), 1)
    ii = lax.broadcasted_iota(jnp.int32, (C, C), 0)
    jj = lax.broadcasted_iota(jnp.int32, (C, C), 1)
    causal = ii >= jj
    ones_blk = jnp.ones((C, LANES), BF16)
    zero_rows = jnp.zeros((DK, 2 * LANES), BF16)

    for hh in range(2):
        tile = slice(hh * LANES, (hh + 1) * LANES)
        half = ((lane >= hh * DK) & (lane < (hh + 1) * DK)).astype(BF16)
        bi = bi_ref[hh]
        bfg = bf_ref[hh]
        i_r = _softcap(ir_ref[hh] + bi)
        f_r = _log_sigmoid(_softcap(fr_ref[hh] + bfg))
        i_c = _softcap(ic_ref[hh] + bi[:, :N])
        f_c = _log_sigmoid(_softcap(fc_ref[hh] + bfg[:, :N]))
        a_r = _cumsum_pow2(f_r, 1)
        a_c = _cumsum_pow2(f_c, 0)
        b_r = i_r - a_r
        b_c = i_c - a_c
        a_last = a_c[C - 1:C, :]
        g_loc = jnp.max(a_last + b_c, axis=0, keepdims=True)
        w_end_r = a_r[:, C - 1:C] + b_r
        w_exp_r = jnp.exp(w_end_r - jnp.max(w_end_r, axis=1, keepdims=True)) * (DK ** -0.5)

        m_st = jnp.zeros((1, 1), F32)
        m_prev = jnp.zeros((1, N), F32)
        for n in range(N):
            m_prev = jnp.where(lane_n == n, m_st, m_prev)
            m_st = jnp.maximum(a_last[:, n:n + 1] + m_st, g_loc[:, n:n + 1])
        m_next = jnp.maximum(a_last + m_prev, g_loc)
        s_old = jnp.exp(a_last + m_prev - m_next)
        s_new = jnp.exp(g_loc - m_next)
        m_all = jnp.maximum(_cummax_pow2(b_c, 0), m_prev)
        am_all = a_c + m_all

        for n in range(N):
            r = pl.ds(n * C, C)
            v_aug = jnp.concatenate([v_ref[r, tile], ones_blk], axis=1)
            kwt = (k_ref[r, :].T.astype(F32) * w_exp_r[n:n + 1, :]).astype(BF16)
            kv = jnp.dot(kwt, v_aug, preferred_element_type=F32)
            kv_ref[hh * N + n] = kv[hh * DK:(hh + 1) * DK, :]

        state = jnp.zeros((DK, 2 * LANES), F32)
        for n in range(N):
            st_ref[hh * N + n] = state.astype(BF16)
            state = s_old[:, n:n + 1] * state + s_new[:, n:n + 1] * kv_ref[hh * N + n]

        nw = nw_ref[:, tile]
        for n in range(N):
            r = pl.ds(n * C, C)
            q2 = q_ref[r, :]
            v_aug = jnp.concatenate([v_ref[r, tile], ones_blk], axis=1)
            s = lax.dot_general(q2 * half, k_ref[r, :], (((1,), (1,)), ((), ())),
                                preferred_element_type=F32) * (DK ** -0.5)
            m_bc = jnp.broadcast_to(m_all[:, n:n + 1], (C, LANES))
            am_bc = jnp.broadcast_to(am_all[:, n:n + 1], (C, LANES))
            qkw = s * jnp.where(causal, jnp.exp(b_r[n:n + 1, :] - m_bc), 0.0)
            st = st_ref[hh * N + n]
            st_full = jnp.concatenate([st, zero_rows] if hh == 0 else [zero_rows, st], axis=0)
            intra = jnp.dot(qkw.astype(BF16), v_aug, preferred_element_type=F32)
            inter = jnp.dot(q2, st_full, preferred_element_type=F32)
            s_inter = jnp.exp(m_prev[:, n:n + 1] - m_bc)
            num = intra[:, :LANES] + inter[:, :LANES] * s_inter
            den = intra[:, LANES:] + inter[:, LANES:] * s_inter
            den = jnp.maximum(jnp.abs(den), jnp.exp(-am_bc))
            hid = _rms(num / den, nw, EPS)
            o_ref[r, tile] = (_sigmoid(op_ref[r, tile].astype(F32)) * hid).astype(BF16)


def _mlstm(z, g_rows, g_cols, bias, nw, l, B, S):
    T = B * S
    N = S // ML_CHUNK
    HP = ML_HEADS // 2
    qk = lambda cb: pl.BlockSpec((S, LANES), lambda b, p: (b, cb + p))
    wide = lambda cb: pl.BlockSpec((S, 2 * LANES), lambda b, p: (b, cb // 2 + p))
    return pl.pallas_call(
        _mlstm_kernel,
        grid=(B, HP),
        in_specs=[
            qk(CB_MQ), qk(CB_MK), wide(CB_MV), wide(CB_MO),
            pl.BlockSpec((None, 2, N, ML_CHUNK), lambda b, p: (b, p, 0, 0)),
            pl.BlockSpec((None, 2, N, ML_CHUNK), lambda b, p: (b, HP + p, 0, 0)),
            pl.BlockSpec((None, 2, ML_CHUNK, N), lambda b, p: (b, p, 0, 0)),
            pl.BlockSpec((None, 2, ML_CHUNK, N), lambda b, p: (b, HP + p, 0, 0)),
            pl.BlockSpec((2, 1, LANES), lambda b, p: (p, 0, 0)),
            pl.BlockSpec((2, 1, LANES), lambda b, p: (HP + p, 0, 0)),
            pl.BlockSpec((None, 1, 2 * LANES), lambda b, p: (l, 0, p)),
        ],
        out_specs=pl.BlockSpec((S, 2 * LANES), lambda b, p: (b, p)),
        out_shape=jax.ShapeDtypeStruct((T, ML_HEADS * ML_DV), BF16),
        scratch_shapes=[
            pltpu.VMEM((2 * N, ML_DK, 2 * LANES), F32),
            pltpu.VMEM((2 * N, ML_DK, 2 * LANES), BF16),
        ],
        compiler_params=_cparams("parallel", "parallel"),
        name="mlstm",
    )(z, z, z, z, g_rows, g_rows, g_cols, g_cols, bias, bias, nw)


def _moba_kernel(q_ref, k_ref, v_ref, cos_ref, sin_ref, qw_ref, kw_ref, o_ref, kn_ref, vt_ref):
    S = k_ref.shape[0]
    BLK = MB_BLOCK
    NB = S // BLK
    nt = (((1,), (1,)), ((), ()))
    qw = qw_ref[...]
    kw = kw_ref[...]

    kmeans = []
    for jb in range(NB):
        r = pl.ds(jb * BLK, BLK)
        kn = _rot(_rms(k_ref[r, :].astype(F32), kw, EPS), cos_ref[r, :], sin_ref[r, :])
        kn_ref[r, :] = kn.astype(BF16)
        kmeans.append(jnp.mean(kn, axis=0, keepdims=True))
        vt_ref[:, r] = v_ref[r, :].astype(F32).T.astype(BF16)
    kmean = jnp.concatenate(kmeans, axis=0)

    blk_id = lax.broadcasted_iota(jnp.int32, (NB, BLK), 0)
    kpos = lax.broadcasted_iota(jnp.int32, (BLK, BLK), 0)
    qpos = lax.broadcasted_iota(jnp.int32, (BLK, BLK), 1)
    causal = kpos <= qpos

    for qi in range(NB):
        rq = pl.ds(qi * BLK, BLK)
        qn = _rot(_rms(q_ref[rq, :].astype(F32), qw, EPS), cos_ref[rq, :], sin_ref[rq, :])
        qs = (qn * (MB_DH ** -0.5 * LOG2E)).astype(BF16)
        s_all = lax.dot_general(kn_ref[pl.ds(0, (qi + 1) * BLK), :], qs, nt, preferred_element_type=F32)
        s_own = jnp.where(causal, s_all[qi * BLK:, :], NEG_BIG)
        m = jnp.max(s_own, axis=0, keepdims=True)
        if qi > 0:
            gate = lax.dot_general(kmean, qn, nt, precision=lax.Precision.HIGHEST, preferred_element_type=F32)
            past = blk_id < qi
            chosen = []
            for jb in range(qi):
                gj = gate[jb:jb + 1, :]
                beats = past & ((gate > gj) | ((gate == gj) & (blk_id < jb)))
                rank = jnp.sum(jnp.where(beats, 1.0, 0.0), axis=0, keepdims=True)
                chosen.append(rank < float(MB_TOPK))
            for jb in range(qi):
                cmax = jnp.max(s_all[jb * BLK:(jb + 1) * BLK, :], axis=0, keepdims=True)
                m = jnp.maximum(m, jnp.where(chosen[jb], cmax, NEG_BIG))
        ps = []
        l = jnp.zeros((1, BLK), F32)
        for jb in range(qi):
            pj = jnp.exp2(s_all[jb * BLK:(jb + 1) * BLK, :] - jnp.where(chosen[jb], m, -NEG_BIG))
            l = l + jnp.sum(pj, axis=0, keepdims=True)
            ps.append(pj.astype(BF16))
        p_own = jnp.exp2(s_own - m)
        l = l + jnp.sum(p_own, axis=0, keepdims=True)
        ps.append(p_own.astype(BF16))
        p_all = jnp.concatenate(ps, axis=0) if qi > 0 else ps[0]
        acc = jnp.dot(vt_ref[:, pl.ds(0, (qi + 1) * BLK)], p_all, preferred_element_type=F32)
        o_ref[rq, :] = (acc / l).T.astype(BF16)


def _moba(z, cos, sin, qw, kw, l, B, S):
    T = B * S
    H = MB_HEADS
    blk = lambda cb: pl.BlockSpec((S, LANES), lambda b, h: (b, cb + h))
    return pl.pallas_call(
        _moba_kernel,
        grid=(B, H),
        in_specs=[
            blk(CB_AQ), blk(CB_AK), blk(CB_AV),
            pl.BlockSpec((S, LANES), lambda b, h: (0, 0)),
            pl.BlockSpec((S, LANES), lambda b, h: (0, 0)),
            pl.BlockSpec((None, 1, LANES), lambda b, h: (l, 0, 0)),
            pl.BlockSpec((None, 1, LANES), lambda b, h: (l, 0, 0)),
        ],
        out_specs=pl.BlockSpec((S, LANES), lambda b, h: (b, h)),
        out_shape=jax.ShapeDtypeStruct((T, H * MB_DH), BF16),
        scratch_shapes=[pltpu.VMEM((S, LANES), BF16), pltpu.VMEM((LANES, S), BF16)],
        compiler_params=_cparams("parallel", "parallel"),
        name="moba",
    )(z, z, z, cos, sin, qw, kw)


def _merge_kernel(ya_ref, yb_ref, yc_ref, ga_ref, gb_ref, gc_ref, wa_ref, wb_ref, wc_ref, wo_ref,
                  x_ref, o_ref, *, rb):
    tm = x_ref.shape[0]

    @pl.when(pl.program_id(1) == 0)
    def _():
        o_ref[...] = x_ref[...]

    def up(k):
        rows = pl.ds(k * rb, rb)
        return tuple(jnp.dot(y_ref[rows, :], w_ref[...], preferred_element_type=F32)
                     for y_ref, w_ref in ((ya_ref, wa_ref), (yb_ref, wb_ref), (yc_ref, wc_ref)))

    def finish(k, prods):
        rows = pl.ds(k * rb, rb)
        m = sum(_sigmoid(g_ref[rows, :].astype(F32)) * p
                for g_ref, p in zip((ga_ref, gb_ref, gc_ref), prods))
        o_ref[rows, :] += jnp.dot(m.astype(BF16), wo_ref[...], preferred_element_type=F32)

    pend = None
    for k in range(tm // rb):
        prods = up(k)
        if pend is not None:
            finish(*pend)
        pend = (k, prods)
    finish(*pend)


def _merge(ya, yb, yc, z, wpa, wpb, wpc, wout, x2, l, tm, tc):
    T, D = x2.shape
    W = ya.shape[1]
    yblk = pl.BlockSpec((tm, W), lambda i, c: (i, 0))
    gblk = lambda col: pl.BlockSpec((tm, tc), lambda i, c: (i, col // tc + c))
    wblk = pl.BlockSpec((None, W, tc), lambda i, c: (l, 0, c))
    return pl.pallas_call(
        functools.partial(_merge_kernel, rb=min(RB_MERGE, tm)),
        grid=(T // tm, D // tc),
        in_specs=[
            yblk, yblk, yblk, gblk(COL_GA), gblk(COL_GB), gblk(COL_GC), wblk, wblk, wblk,
            pl.BlockSpec((None, tc, D), lambda i, c: (l, c, 0)),
            pl.BlockSpec((tm, D), lambda i, c: (i, 0)),
        ],
        out_specs=pl.BlockSpec((tm, D), lambda i, c: (i, 0)),
        out_shape=jax.ShapeDtypeStruct((T, D), F32),
        compiler_params=_cparams("parallel", "arbitrary"),
        name="merge",
    )(ya, yb, yc, z, z, z, wpa, wpb, wpc, wout, x2)


def _ffn_kernel(x_ref, g_ref, wa_ref, wb_ref, cwa_ref, cwb_ref, cba_ref, cbb_ref, wd_ref,
                o_ref, h_ref, ta_ref, tb_ref, *, tiles_per_seq, rb):
    i = pl.program_id(0)
    c = pl.program_id(1)
    tm = x_ref.shape[0]
    tc = wa_ref.shape[1]

    @pl.when(c == 0)
    def _():
        x = x_ref[...]
        h_ref[...] = _rms(x, g_ref[...], EPS).astype(BF16)
        o_ref[...] = x

    @pl.when((i % tiles_per_seq) == 0)
    def _():
        ta_ref[c] = jnp.zeros((8, tc), F32)
        tb_ref[c] = jnp.zeros((8, tc), F32)

    row = lax.broadcasted_iota(jnp.int32, (rb, tc), 0)
    cwa = cwa_ref[...]
    cwb = cwb_ref[...]
    cba = cba_ref[...]
    cbb = cbb_ref[...]

    def up(k):
        hs = h_ref[pl.ds(k * rb, rb), :]
        return (jnp.dot(hs, wa_ref[...], preferred_element_type=F32),
                jnp.dot(hs, wb_ref[...], preferred_element_type=F32))

    def conv(u, prev, cw, cb):
        p1 = prev[7:8, :]
        p2 = prev[6:7, :]
        u1 = jnp.where(row == 0, p1, pltpu.roll(u, 1, 0))
        u2 = jnp.where(row == 0, p2, jnp.where(row == 1, p1, pltpu.roll(u, 2, 0)))
        return cb + u2 * cw[0:1, :] + u1 * cw[1:2, :] + u * cw[2:3, :]

    def finish(k, ua, ub, pa, pb):
        a = conv(ua, pa, cwa, cba)
        b = conv(ub, pb, cwb, cbb)
        act = (a * _sigmoid(a) * b).astype(BF16)
        rows = pl.ds(k * rb, rb)
        o_ref[rows, :] += jnp.dot(act, wd_ref[...], preferred_element_type=F32)

    pa, pb = ta_ref[c], tb_ref[c]
    pend = None
    for k in range(tm // rb):
        ua, ub = up(k)
        if pend is not None:
            finish(*pend)
        pend = (k, ua, ub, pa, pb)
        pa, pb = ua[rb - 8:, :], ub[rb - 8:, :]
    finish(*pend)
    ta_ref[c] = pa
    tb_ref[c] = pb


def _ffn(x2, norm_w, w_up, conv_w, conv_b, w_down, l, S, tm, tc):
    T, D = x2.shape
    F = w_down.shape[1]
    nc = F // tc
    wup = lambda off: pl.BlockSpec((None, D, tc), lambda i, c: (l, 0, off + c))
    cw = lambda off: pl.BlockSpec((None, FFN_CONV, tc), lambda i, c: (l, 0, off + c))
    cb = lambda off: pl.BlockSpec((None, 1, tc), lambda i, c: (l, 0, off + c))
    return pl.pallas_call(
        functools.partial(_ffn_kernel, tiles_per_seq=S // tm, rb=min(RB_FFN, tm)),
        grid=(T // tm, nc),
        in_specs=[
            pl.BlockSpec((tm, D), lambda i, c: (i, 0)),
            pl.BlockSpec((None, 1, D), lambda i, c: (l, 0, 0)),
            wup(0), wup(nc), cw(0), cw(nc), cb(0), cb(nc),
            pl.BlockSpec((None, tc, D), lambda i, c: (l, c, 0)),
        ],
        out_specs=pl.BlockSpec((tm, D), lambda i, c: (i, 0)),
        out_shape=jax.ShapeDtypeStruct((T, D), F32),
        scratch_shapes=[
            pltpu.VMEM((tm, D), BF16),
            pltpu.VMEM((nc, 8, tc), F32),
            pltpu.VMEM((nc, 8, tc), F32),
        ],
        compiler_params=_cparams("arbitrary", "arbitrary"),
        name="conv_glu_ffn",
    )(x2, norm_w, w_up, w_up, conv_w, conv_w, conv_b, conv_b, w_down)


def _rot_tables(S, inv_freq):
    ang = jnp.arange(S, dtype=F32)[:, None] * inv_freq[None, :]
    cos = jnp.cos(ang)
    sin = jnp.sin(ang)
    return jnp.concatenate([cos, cos], axis=1), jnp.concatenate([-sin, sin], axis=1)


GATE_COL = 7168
N_GATE = 2 * ML_HEADS
PREP_ROWS, PREP_COLS = 512, 1024


def _prep_kernel(main_ref, next_ref, o_ref):
    @pl.when(pl.program_id(2) < GATE_COL // PREP_COLS)
    def _():
        o_ref[...] = main_ref[...].astype(BF16)

    @pl.when(pl.program_id(2) >= GATE_COL // PREP_COLS)
    def _():
        x = main_ref[...]
        rows, width = x.shape
        rolled = pltpu.roll(x, width - N_GATE, 1)
        tail = pltpu.roll(next_ref[...], LANES - N_GATE, 1)
        lane = lax.broadcasted_iota(jnp.int32, (rows, LANES), 1)
        last = jnp.where(lane >= LANES - N_GATE, tail, rolled[:, width - LANES:])
        o_ref[...] = jnp.concatenate([rolled[:, :width - LANES], last], axis=1).astype(BF16)


def _prep_w_in(w_in):
    L, D, _ = w_in.shape
    pr = min(PREP_ROWS, D)
    next_blk = lambda l, r, j: (l, r, jnp.minimum((j + 1) * (PREP_COLS // LANES), N_MAIN // LANES))
    w_main = pl.pallas_call(
        _prep_kernel,
        grid=(L, D // pr, N_MAIN // PREP_COLS),
        in_specs=[
            pl.BlockSpec((None, pr, PREP_COLS), lambda l, r, j: (l, r, j)),
            pl.BlockSpec((None, pr, LANES), next_blk),
        ],
        out_specs=pl.BlockSpec((None, pr, PREP_COLS), lambda l, r, j: (l, r, j)),
        out_shape=jax.ShapeDtypeStruct((L, D, N_MAIN), BF16),
        compiler_params=_cparams("parallel", "parallel", "parallel"),
        name="prep_w_in",
    )(w_in, w_in)
    w_gate = jnp.pad(w_in[:, :, GATE_COL:GATE_COL + N_GATE], ((0, 0), (0, 0), (0, LANES - N_GATE)))
    return w_main, w_gate.astype(BF16)


TM_PROJ, TN_PROJ = 1024, 2048
TM_MERGE, TC_MERGE = 512, 1024
RB_MERGE = 256
TM_FFN, TC_FFN = 1024, 512
RB_FFN = 256


def kernel(x, w_in, b_ig, b_fg, norm_mix, ret_gn, ml_norm, q_norm, k_norm, w_pa, w_pb, w_pc, w_out,
           norm_ffn, w_up, conv_w, conv_b, w_down):
    B, S, D = x.shape
    T = B * S
    L = w_in.shape[0]

    w_main, w_gate = _prep_w_in(w_in)
    wpa, wpb, wpc, wout = (w.astype(BF16) for w in (w_pa, w_pb, w_pc, w_out))
    wup = w_up.astype(BF16)
    wdown = w_down.astype(BF16)
    norm_mix3 = norm_mix.reshape(L, 1, D)
    norm_ffn3 = norm_ffn.reshape(L, 1, D)
    ret_gn3 = ret_gn.reshape(L, 1, -1)
    ml_norm3 = ml_norm.reshape(L, 1, -1)
    q_norm3 = q_norm.reshape(L, 1, MB_DH)
    k_norm3 = k_norm.reshape(L, 1, MB_DH)
    conv_b3 = conv_b.reshape(L, 1, -1)

    ret_freq = 1.0 / (ROPE_THETA ** jnp.linspace(0.0, 1.0, RET_DK // 2, dtype=F32))
    rope_freq = 1.0 / (ROPE_THETA ** (jnp.arange(0, MB_DH, 2, dtype=F32) / MB_DH))
    ret_cos, ret_sin = _rot_tables(S, ret_freq)
    mb_cos, mb_sin = _rot_tables(S, rope_freq)
    log_gamma = jnp.log1p(-jnp.exp2(-5.0 - jnp.arange(RET_HEADS, dtype=F32)))
    lg = jnp.broadcast_to(log_gamma[:, None, None], (RET_HEADS, 1, LANES))

    N = S // ML_CHUNK
    x2 = x.reshape(T, D)
    for l in range(L):
        z, zg = _in_proj(x2, norm_mix3, w_main, w_gate, l, min(TM_PROJ, S), TN_PROJ)
        gates = zg[:, :2 * ML_HEADS].reshape(B, N, ML_CHUNK, 2 * ML_HEADS)
        g_rows = gates.transpose(0, 3, 1, 2)
        g_cols = gates.transpose(0, 3, 2, 1)
        bias = jnp.concatenate([b_ig[l], b_fg[l]])
        bias = jnp.broadcast_to(bias[:, None, None], (2 * ML_HEADS, 1, LANES))
        y_a = _retention(z, ret_cos, ret_sin, lg, ret_gn3, l, B, S)
        y_b = _mlstm(z, g_rows, g_cols, bias, ml_norm3, l, B, S)
        y_c = _moba(z, mb_cos, mb_sin, q_norm3, k_norm3, l, B, S)
        x2 = _merge(y_a, y_b, y_c, z, wpa, wpb, wpc, wout, x2, l, min(TM_MERGE, S), TC_MERGE)
        x2 = _ffn(x2, norm_ffn3, wup, conv_w, conv_b3, wdown, l, S, min(TM_FFN, S), TC_FFN)
    return x2.reshape(B, S, D)
```

```python
import functools

import jax
import jax.numpy as jnp
from jax import lax
from jax.experimental import pallas as pl
from jax.experimental.pallas import tpu as pltpu

F32 = jnp.float32
BF16 = jnp.bfloat16

D_MODEL = 2048
DEPTH = 4
RET_HEADS = 8
RET_DK = 128
RET_DV = 128
RET_CHUNK = 128
GN_EPS = 1e-5
ML_HEADS = 8
ML_DK = 64
ML_DV = 128
ML_CHUNK = 128
GATE_SOFTCAP = 15.0
MB_HEADS = 8
MB_DH = 128
MB_BLOCK = 256
MB_TOPK = 3
ROPE_THETA = 10000.0
D_FF = 5632
FFN_CONV = 3
EPS = 1e-6

LANES = 128
N_MAIN = 16384
CB_RQ, CB_RK, CB_RV, CB_RG = 0, 8, 16, 24
CB_MQ, CB_MK, CB_MV, CB_MO = 32, 36, 40, 48
CB_AQ, CB_AK, CB_AV = 56, 64, 72
COL_GA, COL_GB, COL_GC = 10240, 12288, 14336
NEG_BIG = -1e30
LOG2E = 1.4426950408889634

VMEM_LIMIT_BYTES = 56 * 1024 * 1024


def _cparams(*sem):
    return pltpu.CompilerParams(dimension_semantics=sem, vmem_limit_bytes=VMEM_LIMIT_BYTES)


def _rms(x, g, eps):
    return x * lax.rsqrt(jnp.mean(x * x, axis=-1, keepdims=True) + eps) * g


def _sigmoid(x):
    return 1.0 / (1.0 + jnp.exp(-x))


def _rot(x, cos, sin_signed):
    return x * cos + pltpu.roll(x, LANES // 2, 1) * sin_signed


_NT = (((1,), (1,)), ((), ()))


def _inproj_kernel(x_ref, g_ref, w_ref, wg_ref, z_ref, zg_ref, h_ref):
    @pl.when(pl.program_id(1) == 0)
    def _():
        hb = _rms(x_ref[...], g_ref[...], EPS).astype(BF16)
        h_ref[...] = hb
        zg_ref[...] = lax.dot_general(hb, wg_ref[...], _NT, preferred_element_type=F32)

    z_ref[...] = lax.dot_general(h_ref[...], w_ref[...], _NT, preferred_element_type=F32).astype(BF16)


def _in_proj(x2, norm_w, w_main, w_gate, l, tm, tn):
    T, D = x2.shape
    N = w_main.shape[1]
    return pl.pallas_call(
        _inproj_kernel,
        grid=(T // tm, N // tn),
        in_specs=[
            pl.BlockSpec((tm, D), lambda i, j: (i, 0)),
            pl.BlockSpec((None, 1, D), lambda i, j: (l, 0, 0)),
            pl.BlockSpec((None, tn, D), lambda i, j: (l, j, 0)),
            pl.BlockSpec((None, LANES, D), lambda i, j: (l, 0, 0)),
        ],
        out_specs=[
            pl.BlockSpec((tm, tn), lambda i, j: (i, j)),
            pl.BlockSpec((tm, LANES), lambda i, j: (i, 0)),
        ],
        out_shape=[jax.ShapeDtypeStruct((T, N), BF16), jax.ShapeDtypeStruct((T, LANES), F32)],
        scratch_shapes=[pltpu.VMEM((tm, D), BF16)],
        compiler_params=_cparams("parallel", "arbitrary"),
        name="in_proj",
    )(x2, norm_w, w_main, w_gate)


def _ret_head(q_ref, k_ref, v_ref, g_ref, cos_ref, sin_ref, lg, gn, o_ref, tile):
    S = q_ref.shape[0]
    C = RET_CHUNK
    ii = lax.broadcasted_iota(jnp.int32, (C, C), 0)
    jj = lax.broadcasted_iota(jnp.int32, (C, C), 1)
    decay = jnp.where(ii >= jj, jnp.exp(lg * jnp.maximum(ii - jj, 0).astype(F32)), 0.0)
    pos = lax.broadcasted_iota(jnp.int32, (C, LANES), 0).astype(F32)
    kdec = jnp.exp((C - 1.0 - pos) * lg)
    qdec = jnp.exp((pos + 1.0) * lg)
    cdec = jnp.exp(float(C) * lg)

    state = jnp.zeros((RET_DK, RET_DV), F32)
    for n in range(S // C):
        r = pl.ds(n * C, C)
        cs = cos_ref[r, :]
        sn = sin_ref[r, :]
        q = _rot(q_ref[r, tile].astype(F32), cs, sn)
        k = _rot(k_ref[r, tile].astype(F32), cs, sn) * (RET_DK ** -0.5)
        v = v_ref[r, tile]
        s = lax.dot_general(q.astype(BF16), k.astype(BF16), _NT, preferred_element_type=F32) * decay
        intra = jnp.dot(s.astype(BF16), v, preferred_element_type=F32)
        inter = jnp.dot((q * qdec).astype(BF16), state.astype(BF16), preferred_element_type=F32)
        kv = lax.dot_general((k * kdec).astype(BF16), v, (((0,), (0,)), ((), ())),
                             preferred_element_type=F32)
        o = intra + inter
        oc = o - jnp.mean(o, axis=-1, keepdims=True)
        on = oc * lax.rsqrt(jnp.mean(oc * oc, axis=-1, keepdims=True) + GN_EPS) * gn
        g = g_ref[r, tile].astype(F32)
        o_ref[r, tile] = (g * _sigmoid(g) * on).astype(BF16)
        state = state * cdec + kv
        yield


def _softcap(x):
    return GATE_SOFTCAP * jnp.tanh(x * (1.0 / GATE_SOFTCAP))


def _log_sigmoid(x):
    return jnp.minimum(x, 0.0) - jnp.log(1.0 + jnp.exp(-jnp.abs(x)))


def _cumsum_pow2(x, axis):
    n = x.shape[axis]
    idx = lax.broadcasted_iota(jnp.int32, x.shape, axis)
    s = 1
    while s < n:
        x = x + jnp.where(idx >= s, pltpu.roll(x, s, axis), 0.0)
        s *= 2
    return x


def _cummax_pow2(x, axis):
    n = x.shape[axis]
    idx = lax.broadcasted_iota(jnp.int32, x.shape, axis)
    s = 1
    while s < n:
        x = jnp.maximum(x, jnp.where(idx >= s, pltpu.roll(x, s, axis), NEG_BIG))
        s *= 2
    return x


def _mlstm_pair(q_ref, k_ref, v_ref, op_ref, ir_ref, fr_ref, ic_ref, fc_ref, bi_ref, bf_ref, nw_ref, o_ref,
                kv_ref, st_ref):
    S = q_ref.shape[0]
    C = ML_CHUNK
    N = S // C
    DK = ML_DK
    lane = lax.broadcasted_iota(jnp.int32, (1, LANES), 1)
    lane_n = lax.broadcasted_iota(jnp.int32, (1, N), 1)
    ii = lax.broadcasted_iota(jnp.int32, (C, C), 0)
    jj = lax.broadcasted_iota(jnp.int32, (C, C), 1)
    causal = ii >= jj
    ones_blk = jnp.ones((C, LANES), BF16)
    zero_rows = jnp.zeros((DK, 2 * LANES), BF16)

    for hh in range(2):
        tile = slice(hh * LANES, (hh + 1) * LANES)
        half = ((lane >= hh * DK) & (lane < (hh + 1) * DK)).astype(BF16)
        bi = bi_ref[hh]
        bfg = bf_ref[hh]
        i_r = _softcap(ir_ref[hh] + bi)
        f_r = _log_sigmoid(_softcap(fr_ref[hh] + bfg))
        i_c = _softcap(ic_ref[hh] + bi[:, :N])
        f_c = _log_sigmoid(_softcap(fc_ref[hh] + bfg[:, :N]))
        a_r = _cumsum_pow2(f_r, 1)
        a_c = _cumsum_pow2(f_c, 0)
        b_r = i_r - a_r
        b_c = i_c - a_c
        a_last = a_c[C - 1:C, :]
        g_loc = jnp.max(a_last + b_c, axis=0, keepdims=True)
        w_end_r = a_r[:, C - 1:C] + b_r
        w_exp_r = jnp.exp(w_end_r - jnp.max(w_end_r, axis=1, keepdims=True)) * (DK ** -0.5)

        m_st = jnp.zeros((1, 1), F32)
        m_prev = jnp.zeros((1, N), F32)
        for n in range(N):
            m_prev = jnp.where(lane_n == n, m_st, m_prev)
            m_st = jnp.maximum(a_last[:, n:n + 1] + m_st, g_loc[:, n:n + 1])
        m_next = jnp.maximum(a_last + m_prev, g_loc)
        s_old = jnp.exp(a_last + m_prev - m_next)
        s_new = jnp.exp(g_loc - m_next)
        m_all = jnp.maximum(_cummax_pow2(b_c, 0), m_prev)
        am_all = a_c + m_all
        yield

        for n in range(N):
            r = pl.ds(n * C, C)
            v_aug = jnp.concatenate([v_ref[r, tile], ones_blk], axis=1)
            kwt = (k_ref[r, :].T.astype(F32) * w_exp_r[n:n + 1, :]).astype(BF16)
            kv = jnp.dot(kwt, v_aug, preferred_element_type=F32)
            kv_ref[hh * N + n] = kv[hh * DK:(hh + 1) * DK, :]
            yield

        state = jnp.zeros((DK, 2 * LANES), F32)
        for n in range(N):
            st_ref[hh * N + n] = state.astype(BF16)
            state = s_old[:, n:n + 1] * state + s_new[:, n:n + 1] * kv_ref[hh * N + n]
        yield

        nw = nw_ref[:, tile]
        for n in range(N):
            r = pl.ds(n * C, C)
            q2 = q_ref[r, :]
            v_aug = jnp.concatenate([v_ref[r, tile], ones_blk], axis=1)
            s = lax.dot_general(q2 * half, k_ref[r, :], _NT, preferred_element_type=F32) * (DK ** -0.5)
            m_bc = jnp.broadcast_to(m_all[:, n:n + 1], (C, LANES))
            am_bc = jnp.broadcast_to(am_all[:, n:n + 1], (C, LANES))
            qkw = s * jnp.where(causal, jnp.exp(b_r[n:n + 1, :] - m_bc), 0.0)
            st = st_ref[hh * N + n]
            st_full = jnp.concatenate([st, zero_rows] if hh == 0 else [zero_rows, st], axis=0)
            intra = jnp.dot(qkw.astype(BF16), v_aug, preferred_element_type=F32)
            inter = jnp.dot(q2, st_full, preferred_element_type=F32)
            s_inter = jnp.exp(m_prev[:, n:n + 1] - m_bc)
            num = intra[:, :LANES] + inter[:, :LANES] * s_inter
            den = intra[:, LANES:] + inter[:, LANES:] * s_inter
            den = jnp.maximum(jnp.abs(den), jnp.exp(-am_bc))
            hid = _rms(num / den, nw, EPS)
            o_ref[r, tile] = (_sigmoid(op_ref[r, tile].astype(F32)) * hid).astype(BF16)
            yield


def _moba_head(q_ref, k_ref, v_ref, cos_ref, sin_ref, qw, kw, o_ref, kn_ref, vt_ref, tile):
    S = k_ref.shape[0]
    BLK = MB_BLOCK
    NB = S // BLK

    kmeans = []
    for jb in range(NB):
        r = pl.ds(jb * BLK, BLK)
        kn = _rot(_rms(k_ref[r, tile].astype(F32), kw, EPS), cos_ref[r, :], sin_ref[r, :])
        kn_ref[r, :] = kn.astype(BF16)
        kmeans.append(jnp.mean(kn, axis=0, keepdims=True))
        vt_ref[:, r] = v_ref[r, tile].astype(F32).T.astype(BF16)
        yield
    kmean = jnp.concatenate(kmeans, axis=0)

    blk_id = lax.broadcasted_iota(jnp.int32, (NB, BLK), 0)
    kpos = lax.broadcasted_iota(jnp.int32, (BLK, BLK), 0)
    qpos = lax.broadcasted_iota(jnp.int32, (BLK, BLK), 1)
    causal = kpos <= qpos

    for qi in range(NB):
        rq = pl.ds(qi * BLK, BLK)
        qn = _rot(_rms(q_ref[rq, tile].astype(F32), qw, EPS), cos_ref[rq, :], sin_ref[rq, :])
        qs = (qn * (MB_DH ** -0.5 * LOG2E)).astype(BF16)
        s_all = lax.dot_general(kn_ref[pl.ds(0, (qi + 1) * BLK), :], qs, _NT, preferred_element_type=F32)
        s_own = jnp.where(causal, s_all[qi * BLK:, :], NEG_BIG)
        m = jnp.max(s_own, axis=0, keepdims=True)
        if qi > 0:
            gate = lax.dot_general(kmean, qn, _NT, precision=lax.Precision.HIGHEST, preferred_element_type=F32)
            past = blk_id < qi
            chosen = []
            for jb in range(qi):
                gj = gate[jb:jb + 1, :]
                beats = past & ((gate > gj) | ((gate == gj) & (blk_id < jb)))
                rank = jnp.sum(jnp.where(beats, 1.0, 0.0), axis=0, keepdims=True)
                chosen.append(rank < float(MB_TOPK))
            for jb in range(qi):
                cmax = jnp.max(s_all[jb * BLK:(jb + 1) * BLK, :], axis=0, keepdims=True)
                m = jnp.maximum(m, jnp.where(chosen[jb], cmax, NEG_BIG))
        ps = []
        l = jnp.zeros((1, BLK), F32)
        for jb in range(qi):
            pj = jnp.exp2(s_all[jb * BLK:(jb + 1) * BLK, :] - jnp.where(chosen[jb], m, -NEG_BIG))
            l = l + jnp.sum(pj, axis=0, keepdims=True)
            ps.append(pj.astype(BF16))
        p_own = jnp.exp2(s_own - m)
        l = l + jnp.sum(p_own, axis=0, keepdims=True)
        ps.append(p_own.astype(BF16))
        p_all = jnp.concatenate(ps, axis=0) if qi > 0 else ps[0]
        acc = jnp.dot(vt_ref[:, pl.ds(0, (qi + 1) * BLK)], p_all, preferred_element_type=F32)
        o_ref[rq, tile] = (acc / l).T.astype(BF16)
        yield


def _interleave(streams):
    order = sorted(((k + 0.5) / n, i) for i, (_, n) in enumerate(streams) for k in range(n))
    for _, i in order:
        next(streams[i][0])
    for gen, _ in streams:
        assert next(gen, None) is None


def _mixers_kernel(rq_ref, rk_ref, rv_ref, rg_ref, rcos_ref, rsin_ref, lg_ref, gn_ref,
                   mq_ref, mk_ref, mv_ref, mo_ref, ir_ref, fr_ref, ic_ref, fc_ref, bi_ref, bf_ref, nw_ref,
                   aq_ref, ak_ref, av_ref, acos_ref, asin_ref, qw_ref, kw_ref,
                   ya_ref, yb_ref, yc_ref, kv_ref, st_ref, kn_ref, vt_ref):
    S = rq_ref.shape[0]
    tiles = [slice(0, LANES), slice(LANES, 2 * LANES)]
    qw = qw_ref[...]
    kw = kw_ref[...]
    n_ret = S // RET_CHUNK
    n_ml = 2 * (2 + 2 * (S // ML_CHUNK))
    n_mb = 2 * (S // MB_BLOCK)
    streams = []
    for hd in range(2):
        streams.append((_moba_head(aq_ref, ak_ref, av_ref, acos_ref, asin_ref, qw, kw, yc_ref,
                                   kn_ref.at[hd], vt_ref.at[hd], tiles[hd]), n_mb))
        streams.append((_ret_head(rq_ref, rk_ref, rv_ref, rg_ref, rcos_ref, rsin_ref, lg_ref[hd],
                                  gn_ref[:, tiles[hd]], ya_ref, tiles[hd]), n_ret))
    streams.append((_mlstm_pair(mq_ref, mk_ref, mv_ref, mo_ref, ir_ref, fr_ref, ic_ref, fc_ref, bi_ref, bf_ref,
                                nw_ref, yb_ref, kv_ref, st_ref), n_ml))
    _interleave(streams)


def _mixers(z, g_rows, g_cols, bias, ret_cos, ret_sin, lg, ret_gn, ml_norm, mb_cos, mb_sin, qw, kw, l, B, S):
    T = B * S
    N = S // ML_CHUNK
    HP = ML_HEADS // 2
    W2 = 2 * LANES
    wide = lambda cb: pl.BlockSpec((S, W2), lambda b, p: (b, cb // 2 + p))
    narrow = lambda cb: pl.BlockSpec((S, LANES), lambda b, p: (b, cb + p))
    table = pl.BlockSpec((S, LANES), lambda b, p: (0, 0))
    pair_w = pl.BlockSpec((None, 1, W2), lambda b, p: (l, 0, p))
    head_w = pl.BlockSpec((None, 1, LANES), lambda b, p: (l, 0, 0))
    out = pl.BlockSpec((S, W2), lambda b, p: (b, p))
    y_shape = jax.ShapeDtypeStruct((T, ML_HEADS * ML_DV), BF16)
    return pl.pallas_call(
        _mixers_kernel,
        grid=(B, HP),
        in_specs=[
            wide(CB_RQ), wide(CB_RK), wide(CB_RV), wide(CB_RG), table, table,
            pl.BlockSpec((2, 1, LANES), lambda b, p: (p, 0, 0)), pair_w,
            narrow(CB_MQ), narrow(CB_MK), wide(CB_MV), wide(CB_MO),
            pl.BlockSpec((None, 2, N, ML_CHUNK), lambda b, p: (b, p, 0, 0)),
            pl.BlockSpec((None, 2, N, ML_CHUNK), lambda b, p: (b, HP + p, 0, 0)),
            pl.BlockSpec((None, 2, ML_CHUNK, N), lambda b, p: (b, p, 0, 0)),
            pl.BlockSpec((None, 2, ML_CHUNK, N), lambda b, p: (b, HP + p, 0, 0)),
            pl.BlockSpec((2, 1, LANES), lambda b, p: (p, 0, 0)),
            pl.BlockSpec((2, 1, LANES), lambda b, p: (HP + p, 0, 0)),
            pair_w,
            wide(CB_AQ), wide(CB_AK), wide(CB_AV), table, table, head_w, head_w,
        ],
        out_specs=[out, out, out],
        out_shape=[y_shape, y_shape, y_shape],
        scratch_shapes=[
            pltpu.VMEM((2 * N, ML_DK, W2), F32),
            pltpu.VMEM((2 * N, ML_DK, W2), BF16),
            pltpu.VMEM((2, S, LANES), BF16),
            pltpu.VMEM((2, LANES, S), BF16),
        ],
        compiler_params=_cparams("parallel", "parallel"),
        name="mixers",
    )(z, z, z, z, ret_cos, ret_sin, lg, ret_gn,
      z, z, z, z, g_rows, g_rows, g_cols, g_cols, bias, bias, ml_norm,
      z, z, z, mb_cos, mb_sin, qw, kw)


def _merge_kernel(ya_ref, yb_ref, yc_ref, ga_ref, gb_ref, gc_ref, wa_ref, wb_ref, wc_ref, wo_ref,
                  x_ref, o_ref, *, rb):
    tm = x_ref.shape[0]

    @pl.when(pl.program_id(1) == 0)
    def _():
        o_ref[...] = x_ref[...]

    def up(k):
        rows = pl.ds(k * rb, rb)
        return tuple(jnp.dot(y_ref[rows, :], w_ref[...], preferred_element_type=F32)
                     for y_ref, w_ref in ((ya_ref, wa_ref), (yb_ref, wb_ref), (yc_ref, wc_ref)))

    def finish(k, prods):
        rows = pl.ds(k * rb, rb)
        m = sum(_sigmoid(g_ref[rows, :].astype(F32)) * p
                for g_ref, p in zip((ga_ref, gb_ref, gc_ref), prods))
        o_ref[rows, :] += jnp.dot(m.astype(BF16), wo_ref[...], preferred_element_type=F32)

    pend = None
    for k in range(tm // rb):
        prods = up(k)
        if pend is not None:
            finish(*pend)
        pend = (k, prods)
    finish(*pend)


def _merge(ya, yb, yc, z, wpa, wpb, wpc, wout, x2, l, tm, tc):
    T, D = x2.shape
    W = ya.shape[1]
    yblk = pl.BlockSpec((tm, W), lambda i, c: (i, 0))
    gblk = lambda col: pl.BlockSpec((tm, tc), lambda i, c: (i, col // tc + c))
    wblk = pl.BlockSpec((None, W, tc), lambda i, c: (l, 0, c))
    return pl.pallas_call(
        functools.partial(_merge_kernel, rb=min(RB_MERGE, tm)),
        grid=(T // tm, D // tc),
        in_specs=[
            yblk, yblk, yblk, gblk(COL_GA), gblk(COL_GB), gblk(COL_GC), wblk, wblk, wblk,
            pl.BlockSpec((None, tc, D), lambda i, c: (l, c, 0)),
            pl.BlockSpec((tm, D), lambda i, c: (i, 0)),
        ],
        out_specs=pl.BlockSpec((tm, D), lambda i, c: (i, 0)),
        out_shape=jax.ShapeDtypeStruct((T, D), F32),
        compiler_params=_cparams("parallel", "arbitrary"),
        name="merge",
    )(ya, yb, yc, z, z, z, wpa, wpb, wpc, wout, x2)


def _ffn_kernel(x_ref, g_ref, wa_ref, wb_ref, cwa_ref, cwb_ref, cba_ref, cbb_ref, wd_ref,
                o_ref, h_ref, ta_ref, tb_ref, *, tiles_per_seq, rb):
    i = pl.program_id(0)
    c = pl.program_id(1)
    tm = x_ref.shape[0]
    tc = wa_ref.shape[1]

    @pl.when(c == 0)
    def _():
        x = x_ref[...]
        h_ref[...] = _rms(x, g_ref[...], EPS).astype(BF16)
        o_ref[...] = x

    @pl.when((i % tiles_per_seq) == 0)
    def _():
        ta_ref[c] = jnp.zeros((8, tc), F32)
        tb_ref[c] = jnp.zeros((8, tc), F32)

    row = lax.broadcasted_iota(jnp.int32, (rb, tc), 0)
    cwa = cwa_ref[...]
    cwb = cwb_ref[...]
    cba = cba_ref[...]
    cbb = cbb_ref[...]

    def up(k):
        hs = h_ref[pl.ds(k * rb, rb), :]
        return (jnp.dot(hs, wa_ref[...], preferred_element_type=F32),
                jnp.dot(hs, wb_ref[...], preferred_element_type=F32))

    def conv(u, prev, cw, cb):
        p1 = prev[7:8, :]
        p2 = prev[6:7, :]
        u1 = jnp.where(row == 0, p1, pltpu.roll(u, 1, 0))
        u2 = jnp.where(row == 0, p2, jnp.where(row == 1, p1, pltpu.roll(u, 2, 0)))
        return cb + u2 * cw[0:1, :] + u1 * cw[1:2, :] + u * cw[2:3, :]

    def finish(k, ua, ub, pa, pb):
        a = conv(ua, pa, cwa, cba)
        b = conv(ub, pb, cwb, cbb)
        act = (a * _sigmoid(a) * b).astype(BF16)
        rows = pl.ds(k * rb, rb)
        o_ref[rows, :] += jnp.dot(act, wd_ref[...], preferred_element_type=F32)

    pa, pb = ta_ref[c], tb_ref[c]
    pend = None
    for k in range(tm // rb):
        ua, ub = up(k)
        if pend is not None:
            finish(*pend)
        pend = (k, ua, ub, pa, pb)
        pa, pb = ua[rb - 8:, :], ub[rb - 8:, :]
    finish(*pend)
    ta_ref[c] = pa
    tb_ref[c] = pb


def _ffn(x2, norm_w, w_up, conv_w, conv_b, w_down, l, S, tm, tc):
    T, D = x2.shape
    F = w_down.shape[1]
    nc = F // tc
    wup = lambda off: pl.BlockSpec((None, D, tc), lambda i, c: (l, 0, off + c))
    cw = lambda off: pl.BlockSpec((None, FFN_CONV, tc), lambda i, c: (l, 0, off + c))
    cb = lambda off: pl.BlockSpec((None, 1, tc), lambda i, c: (l, 0, off + c))
    return pl.pallas_call(
        functools.partial(_ffn_kernel, tiles_per_seq=S // tm, rb=min(RB_FFN, tm)),
        grid=(T // tm, nc),
        in_specs=[
            pl.BlockSpec((tm, D), lambda i, c: (i, 0)),
            pl.BlockSpec((None, 1, D), lambda i, c: (l, 0, 0)),
            wup(0), wup(nc), cw(0), cw(nc), cb(0), cb(nc),
            pl.BlockSpec((None, tc, D), lambda i, c: (l, c, 0)),
        ],
        out_specs=pl.BlockSpec((tm, D), lambda i, c: (i, 0)),
        out_shape=jax.ShapeDtypeStruct((T, D), F32),
        scratch_shapes=[
            pltpu.VMEM((tm, D), BF16),
            pltpu.VMEM((nc, 8, tc), F32),
            pltpu.VMEM((nc, 8, tc), F32),
        ],
        compiler_params=_cparams("arbitrary", "arbitrary"),
        name="conv_glu_ffn",
    )(x2, norm_w, w_up, w_up, conv_w, conv_w, conv_b, conv_b, w_down)


def _rot_tables(S, inv_freq):
    ang = jnp.arange(S, dtype=F32)[:, None] * inv_freq[None, :]
    cos = jnp.cos(ang)
    sin = jnp.sin(ang)
    return jnp.concatenate([cos, cos], axis=1), jnp.concatenate([-sin, sin], axis=1)


GATE_COL = 7168
N_GATE = 2 * ML_HEADS


def _prep_w_in(w_in):
    w_t = jnp.swapaxes(w_in, 1, 2)
    w_main = jnp.concatenate([w_t[:, :GATE_COL], w_t[:, GATE_COL + N_GATE:]], axis=1).astype(BF16)
    w_gate = jnp.pad(w_t[:, GATE_COL:GATE_COL + N_GATE], ((0, 0), (0, LANES - N_GATE), (0, 0))).astype(BF16)
    return w_main, w_gate


TM_PROJ, TN_PROJ = 1024, 2048
TM_MERGE, TC_MERGE = 512, 1024
RB_MERGE = 256
TM_FFN, TC_FFN = 1024, 512
RB_FFN = 256


def kernel(x, w_in, b_ig, b_fg, norm_mix, ret_gn, ml_norm, q_norm, k_norm, w_pa, w_pb, w_pc, w_out,
           norm_ffn, w_up, conv_w, conv_b, w_down):
    B, S, D = x.shape
    T = B * S
    L = w_in.shape[0]

    w_main, w_gate = _prep_w_in(w_in)
    wpa, wpb, wpc, wout = (w.astype(BF16) for w in (w_pa, w_pb, w_pc, w_out))
    wup = w_up.astype(BF16)
    wdown = w_down.astype(BF16)
    norm_mix3 = norm_mix.reshape(L, 1, D)
    norm_ffn3 = norm_ffn.reshape(L, 1, D)
    ret_gn3 = ret_gn.reshape(L, 1, -1)
    ml_norm3 = ml_norm.reshape(L, 1, -1)
    q_norm3 = q_norm.reshape(L, 1, MB_DH)
    k_norm3 = k_norm.reshape(L, 1, MB_DH)
    conv_b3 = conv_b.reshape(L, 1, -1)

    ret_freq = 1.0 / (ROPE_THETA ** jnp.linspace(0.0, 1.0, RET_DK // 2, dtype=F32))
    rope_freq = 1.0 / (ROPE_THETA ** (jnp.arange(0, MB_DH, 2, dtype=F32) / MB_DH))
    ret_cos, ret_sin = _rot_tables(S, ret_freq)
    mb_cos, mb_sin = _rot_tables(S, rope_freq)
    log_gamma = jnp.log1p(-jnp.exp2(-5.0 - jnp.arange(RET_HEADS, dtype=F32)))
    lg = jnp.broadcast_to(log_gamma[:, None, None], (RET_HEADS, 1, LANES))

    N = S // ML_CHUNK
    x2 = x.reshape(T, D)
    for l in range(L):
        z, zg = _in_proj(x2, norm_mix3, w_main, w_gate, l, min(TM_PROJ, S), TN_PROJ)
        gates = zg[:, :2 * ML_HEADS].reshape(B, N, ML_CHUNK, 2 * ML_HEADS)
        g_rows = gates.transpose(0, 3, 1, 2)
        g_cols = gates.transpose(0, 3, 2, 1)
        bias = jnp.concatenate([b_ig[l], b_fg[l]])
        bias = jnp.broadcast_to(bias[:, None, None], (2 * ML_HEADS, 1, LANES))
        y_a, y_b, y_c = _mixers(z, g_rows, g_cols, bias, ret_cos, ret_sin, lg, ret_gn3, ml_norm3,
                                mb_cos, mb_sin, q_norm3, k_norm3, l, B, S)
        x2 = _merge(y_a, y_b, y_c, z, wpa, wpb, wpc, wout, x2, l, min(TM_MERGE, S), TC_MERGE)
        x2 = _ffn(x2, norm_ffn3, wup, conv_w, conv_b3, wdown, l, S, min(TM_FFN, S), TC_FFN)
    return x2.reshape(B, S, D)
```

```python
import functools

import jax
import jax.numpy as jnp
from jax import lax
from jax.experimental import pallas as pl
from jax.experimental.pallas import tpu as pltpu

F32 = jnp.float32
BF16 = jnp.bfloat16

D_MODEL = 2048
DEPTH = 4
RET_HEADS = 8
RET_DK = 128
RET_DV = 128
RET_CHUNK = 128
GN_EPS = 1e-5
ML_HEADS = 8
ML_DK = 64
ML_DV = 128
ML_CHUNK = 128
GATE_SOFTCAP = 15.0
MB_HEADS = 8
MB_DH = 128
MB_BLOCK = 256
MB_TOPK = 3
ROPE_THETA = 10000.0
D_FF = 5632
FFN_CONV = 3
EPS = 1e-6

LANES = 128
N_IN = 16400
N_MAIN = 16384
GATE_COL = 7168
N_GATE = 2 * ML_HEADS
WIN_PROJ = 1024
assert GATE_COL % WIN_PROJ == 0 and (N_IN - GATE_COL - N_GATE) % WIN_PROJ == 0


def _zcol(c):
    return c if c < GATE_COL else c - N_GATE


(_O_RQ, _O_RK, _O_RV, _O_RG, _O_MQ, _O_MK, _O_MV, _O_MO, _O_AQ, _O_AK, _O_AV, _O_GA, _O_GB, _O_GC) = (
    0, 1024, 2048, 3072, 4096, 4608, 5120, 6144, 7184, 8208, 9232, 10256, 12304, 14352)
CB_RQ, CB_RK, CB_RV, CB_RG = (_zcol(o) // LANES for o in (_O_RQ, _O_RK, _O_RV, _O_RG))
CB_MQ, CB_MK, CB_MV, CB_MO = (_zcol(o) // LANES for o in (_O_MQ, _O_MK, _O_MV, _O_MO))
CB_AQ, CB_AK, CB_AV = (_zcol(o) // LANES for o in (_O_AQ, _O_AK, _O_AV))
NEG_BIG = -1e30
LOG2E = 1.4426950408889634

VMEM_LIMIT_BYTES = 56 * 1024 * 1024


def _cparams(*sem):
    return pltpu.CompilerParams(dimension_semantics=sem, vmem_limit_bytes=VMEM_LIMIT_BYTES)


def _rms(x, g, eps):
    return x * lax.rsqrt(jnp.mean(x * x, axis=-1, keepdims=True) + eps) * g


def _sigmoid(x):
    return 1.0 / (1.0 + jnp.exp(-x))


def _rot(x, cos, sin_signed):
    return x * cos + pltpu.roll(x, LANES // 2, 1) * sin_signed


_NT = (((1,), (1,)), ((), ()))


def _inproj_kernel(x_ref, g_ref, w0_ref, w1_ref, wg_ref, z_ref, zg_ref, h_ref):
    @pl.when(pl.program_id(1) == 0)
    def _():
        hb = _rms(x_ref[...], g_ref[...], EPS).astype(BF16)
        h_ref[...] = hb
        zg_ref[...] = lax.dot_general(hb, wg_ref[...], _NT, preferred_element_type=F32)

    h = h_ref[...]
    z_ref[:, :WIN_PROJ] = lax.dot_general(h, w0_ref[0], _NT, preferred_element_type=F32).astype(BF16)
    z_ref[:, WIN_PROJ:] = lax.dot_general(h, w1_ref[0], _NT, preferred_element_type=F32).astype(BF16)


def _in_proj(x2, norm_w, w_t, w_gate, l, tm):
    T, D = x2.shape

    def window(half):
        def index(i, j):
            start = (2 * j + half) * WIN_PROJ
            return (l, pl.multiple_of(jnp.where(start < GATE_COL, start, start + N_GATE), N_GATE), 0)

        return pl.BlockSpec((pl.Element(1), pl.Element(WIN_PROJ), pl.Element(D)), index)

    return pl.pallas_call(
        _inproj_kernel,
        grid=(T // tm, N_MAIN // (2 * WIN_PROJ)),
        in_specs=[
            pl.BlockSpec((tm, D), lambda i, j: (i, 0)),
            pl.BlockSpec((None, 1, D), lambda i, j: (l, 0, 0)),
            window(0), window(1),
            pl.BlockSpec((None, LANES, D), lambda i, j: (l, 0, 0)),
        ],
        out_specs=[
            pl.BlockSpec((tm, 2 * WIN_PROJ), lambda i, j: (i, j)),
            pl.BlockSpec((tm, LANES), lambda i, j: (i, 0)),
        ],
        out_shape=[jax.ShapeDtypeStruct((T, N_MAIN), BF16), jax.ShapeDtypeStruct((T, LANES), F32)],
        scratch_shapes=[pltpu.VMEM((tm, D), BF16)],
        compiler_params=_cparams("parallel", "arbitrary"),
        name="in_proj",
    )(x2, norm_w, w_t, w_t, w_gate)


def _ret_head(q_ref, k_ref, v_ref, g_ref, cos_ref, sin_ref, lg, gn, o_ref, tile):
    S = q_ref.shape[0]
    C = RET_CHUNK
    ii = lax.broadcasted_iota(jnp.int32, (C, C), 0)
    jj = lax.broadcasted_iota(jnp.int32, (C, C), 1)
    decay = jnp.where(ii >= jj, jnp.exp(lg * jnp.maximum(ii - jj, 0).astype(F32)), 0.0)
    pos = lax.broadcasted_iota(jnp.int32, (C, LANES), 0).astype(F32)
    kdec = jnp.exp((C - 1.0 - pos) * lg)
    qdec = jnp.exp((pos + 1.0) * lg)
    cdec = jnp.exp(float(C) * lg)

    state = jnp.zeros((RET_DK, RET_DV), F32)
    for n in range(S // C):
        r = pl.ds(n * C, C)
        cs = cos_ref[r, :]
        sn = sin_ref[r, :]
        q = _rot(q_ref[r, tile].astype(F32), cs, sn)
        k = _rot(k_ref[r, tile].astype(F32), cs, sn) * (RET_DK ** -0.5)
        v = v_ref[r, tile]
        s = lax.dot_general(q.astype(BF16), k.astype(BF16), _NT, preferred_element_type=F32) * decay
        intra = jnp.dot(s.astype(BF16), v, preferred_element_type=F32)
        inter = jnp.dot((q * qdec).astype(BF16), state.astype(BF16), preferred_element_type=F32)
        kv = lax.dot_general((k * kdec).astype(BF16), v, (((0,), (0,)), ((), ())),
                             preferred_element_type=F32)
        o = intra + inter
        oc = o - jnp.mean(o, axis=-1, keepdims=True)
        on = oc * lax.rsqrt(jnp.mean(oc * oc, axis=-1, keepdims=True) + GN_EPS) * gn
        g = g_ref[r, tile].astype(F32)
        o_ref[r, tile] = (g * _sigmoid(g) * on).astype(BF16)
        state = state * cdec + kv
        yield


def _softcap(x):
    return GATE_SOFTCAP * jnp.tanh(x * (1.0 / GATE_SOFTCAP))


def _log_sigmoid(x):
    return jnp.minimum(x, 0.0) - jnp.log(1.0 + jnp.exp(-jnp.abs(x)))


def _cumsum_pow2(x, axis):
    n = x.shape[axis]
    idx = lax.broadcasted_iota(jnp.int32, x.shape, axis)
    s = 1
    while s < n:
        x = x + jnp.where(idx >= s, pltpu.roll(x, s, axis), 0.0)
        s *= 2
    return x


def _cummax_pow2(x, axis):
    n = x.shape[axis]
    idx = lax.broadcasted_iota(jnp.int32, x.shape, axis)
    s = 1
    while s < n:
        x = jnp.maximum(x, jnp.where(idx >= s, pltpu.roll(x, s, axis), NEG_BIG))
        s *= 2
    return x


def _mlstm_pair(q_ref, k_ref, v_ref, op_ref, ir_ref, fr_ref, ic_ref, fc_ref, bi_ref, bf_ref, nw_ref, o_ref,
                kv_ref, st_ref):
    S = q_ref.shape[0]
    C = ML_CHUNK
    N = S // C
    DK = ML_DK
    lane = lax.broadcasted_iota(jnp.int32, (1, LANES), 1)
    lane_n = lax.broadcasted_iota(jnp.int32, (1, N), 1)
    ii = lax.broadcasted_iota(jnp.int32, (C, C), 0)
    jj = lax.broadcasted_iota(jnp.int32, (C, C), 1)
    causal = ii >= jj
    ones_blk = jnp.ones((C, LANES), BF16)
    zero_rows = jnp.zeros((DK, 2 * LANES), BF16)

    for hh in range(2):
        tile = slice(hh * LANES, (hh + 1) * LANES)
        half = ((lane >= hh * DK) & (lane < (hh + 1) * DK)).astype(BF16)
        bi = bi_ref[hh]
        bfg = bf_ref[hh]
        i_r = _softcap(ir_ref[hh] + bi)
        f_r = _log_sigmoid(_softcap(fr_ref[hh] + bfg))
        i_c = _softcap(ic_ref[hh] + bi[:, :N])
        f_c = _log_sigmoid(_softcap(fc_ref[hh] + bfg[:, :N]))
        a_r = _cumsum_pow2(f_r, 1)
        a_c = _cumsum_pow2(f_c, 0)
        b_r = i_r - a_r
        b_c = i_c - a_c
        a_last = a_c[C - 1:C, :]
        g_loc = jnp.max(a_last + b_c, axis=0, keepdims=True)
        w_end_r = a_r[:, C - 1:C] + b_r
        w_exp_r = jnp.exp(w_end_r - jnp.max(w_end_r, axis=1, keepdims=True)) * (DK ** -0.5)

        m_st = jnp.zeros((1, 1), F32)
        m_prev = jnp.zeros((1, N), F32)
        for n in range(N):
            m_prev = jnp.where(lane_n == n, m_st, m_prev)
            m_st = jnp.maximum(a_last[:, n:n + 1] + m_st, g_loc[:, n:n + 1])
        m_next = jnp.maximum(a_last + m_prev, g_loc)
        s_old = jnp.exp(a_last + m_prev - m_next)
        s_new = jnp.exp(g_loc - m_next)
        m_all = jnp.maximum(_cummax_pow2(b_c, 0), m_prev)
        am_all = a_c + m_all
        yield

        for n in range(N):
            r = pl.ds(n * C, C)
            v_aug = jnp.concatenate([v_ref[r, tile], ones_blk], axis=1)
            kwt = (k_ref[r, :].T.astype(F32) * w_exp_r[n:n + 1, :]).astype(BF16)
            kv = jnp.dot(kwt, v_aug, preferred_element_type=F32)
            kv_ref[hh * N + n] = kv[hh * DK:(hh + 1) * DK, :]
            yield

        state = jnp.zeros((DK, 2 * LANES), F32)
        for n in range(N):
            st_ref[hh * N + n] = state.astype(BF16)
            state = s_old[:, n:n + 1] * state + s_new[:, n:n + 1] * kv_ref[hh * N + n]
        yield

        nw = nw_ref[:, tile]
        for n in range(N):
            r = pl.ds(n * C, C)
            q2 = q_ref[r, :]
            v_aug = jnp.concatenate([v_ref[r, tile], ones_blk], axis=1)
            s = lax.dot_general(q2 * half, k_ref[r, :], _NT, preferred_element_type=F32) * (DK ** -0.5)
            m_bc = jnp.broadcast_to(m_all[:, n:n + 1], (C, LANES))
            am_bc = jnp.broadcast_to(am_all[:, n:n + 1], (C, LANES))
            qkw = s * jnp.where(causal, jnp.exp(b_r[n:n + 1, :] - m_bc), 0.0)
            st = st_ref[hh * N + n]
            st_full = jnp.concatenate([st, zero_rows] if hh == 0 else [zero_rows, st], axis=0)
            intra = jnp.dot(qkw.astype(BF16), v_aug, preferred_element_type=F32)
            inter = jnp.dot(q2, st_full, preferred_element_type=F32)
            s_inter = jnp.exp(m_prev[:, n:n + 1] - m_bc)
            num = intra[:, :LANES] + inter[:, :LANES] * s_inter
            den = intra[:, LANES:] + inter[:, LANES:] * s_inter
            den = jnp.maximum(jnp.abs(den), jnp.exp(-am_bc))
            hid = _rms(num / den, nw, EPS)
            o_ref[r, tile] = (_sigmoid(op_ref[r, tile].astype(F32)) * hid).astype(BF16)
            yield


def _moba_head(q_ref, k_ref, v_ref, cos_ref, sin_ref, qw, kw, o_ref, kn_ref, vt_ref, tile):
    S = k_ref.shape[0]
    BLK = MB_BLOCK
    NB = S // BLK

    kmeans = []
    for jb in range(NB):
        r = pl.ds(jb * BLK, BLK)
        kn = _rot(_rms(k_ref[r, tile].astype(F32), kw, EPS), cos_ref[r, :], sin_ref[r, :])
        kn_ref[r, :] = kn.astype(BF16)
        kmeans.append(jnp.mean(kn, axis=0, keepdims=True))
        vt_ref[:, r] = v_ref[r, tile].astype(F32).T.astype(BF16)
        yield
    kmean = jnp.concatenate(kmeans, axis=0)

    blk_id = lax.broadcasted_iota(jnp.int32, (NB, BLK), 0)
    kpos = lax.broadcasted_iota(jnp.int32, (BLK, BLK), 0)
    qpos = lax.broadcasted_iota(jnp.int32, (BLK, BLK), 1)
    causal = kpos <= qpos

    for qi in range(NB):
        rq = pl.ds(qi * BLK, BLK)
        qn = _rot(_rms(q_ref[rq, tile].astype(F32), qw, EPS), cos_ref[rq, :], sin_ref[rq, :])
        qs = (qn * (MB_DH ** -0.5 * LOG2E)).astype(BF16)
        s_all = lax.dot_general(kn_ref[pl.ds(0, (qi + 1) * BLK), :], qs, _NT, preferred_element_type=F32)
        s_own = jnp.where(causal, s_all[qi * BLK:, :], NEG_BIG)
        m = jnp.max(s_own, axis=0, keepdims=True)
        if qi > 0:
            gate = lax.dot_general(kmean, qn, _NT, precision=lax.Precision.HIGHEST, preferred_element_type=F32)
            past = blk_id < qi
            chosen = []
            for jb in range(qi):
                gj = gate[jb:jb + 1, :]
                beats = past & ((gate > gj) | ((gate == gj) & (blk_id < jb)))
                rank = jnp.sum(jnp.where(beats, 1.0, 0.0), axis=0, keepdims=True)
                chosen.append(rank < float(MB_TOPK))
            for jb in range(qi):
                cmax = jnp.max(s_all[jb * BLK:(jb + 1) * BLK, :], axis=0, keepdims=True)
                m = jnp.maximum(m, jnp.where(chosen[jb], cmax, NEG_BIG))
        ps = []
        l = jnp.zeros((1, BLK), F32)
        for jb in range(qi):
            pj = jnp.exp2(s_all[jb * BLK:(jb + 1) * BLK, :] - jnp.where(chosen[jb], m, -NEG_BIG))
            l = l + jnp.sum(pj, axis=0, keepdims=True)
            ps.append(pj.astype(BF16))
        p_own = jnp.exp2(s_own - m)
        l = l + jnp.sum(p_own, axis=0, keepdims=True)
        ps.append(p_own.astype(BF16))
        p_all = jnp.concatenate(ps, axis=0) if qi > 0 else ps[0]
        acc = jnp.dot(vt_ref[:, pl.ds(0, (qi + 1) * BLK)], p_all, preferred_element_type=F32)
        o_ref[rq, tile] = (acc / l).T.astype(BF16)
        yield


def _interleave(streams):
    order = sorted(((k + 0.5) / n, i) for i, (_, n) in enumerate(streams) for k in range(n))
    for _, i in order:
        next(streams[i][0])
    for gen, _ in streams:
        assert next(gen, None) is None


def _mixers_kernel(rq_ref, rk_ref, rv_ref, rg_ref, rcos_ref, rsin_ref, lg_ref, gn_ref,
                   mq_ref, mk_ref, mv_ref, mo_ref, ir_ref, fr_ref, ic_ref, fc_ref, bi_ref, bf_ref, nw_ref,
                   aq_ref, ak_ref, av_ref, acos_ref, asin_ref, qw_ref, kw_ref,
                   ya_ref, yb_ref, yc_ref, kv_ref, st_ref, kn_ref, vt_ref):
    S = rq_ref.shape[0]
    tiles = [slice(0, LANES), slice(LANES, 2 * LANES)]
    qw = qw_ref[...]
    kw = kw_ref[...]
    n_ret = S // RET_CHUNK
    n_ml = 2 * (2 + 2 * (S // ML_CHUNK))
    n_mb = 2 * (S // MB_BLOCK)
    streams = []
    for hd in range(2):
        streams.append((_moba_head(aq_ref, ak_ref, av_ref, acos_ref, asin_ref, qw, kw, yc_ref,
                                   kn_ref.at[hd], vt_ref.at[hd], tiles[hd]), n_mb))
        streams.append((_ret_head(rq_ref, rk_ref, rv_ref, rg_ref, rcos_ref, rsin_ref, lg_ref[hd],
                                  gn_ref[:, tiles[hd]], ya_ref, tiles[hd]), n_ret))
    streams.append((_mlstm_pair(mq_ref, mk_ref, mv_ref, mo_ref, ir_ref, fr_ref, ic_ref, fc_ref, bi_ref, bf_ref,
                                nw_ref, yb_ref, kv_ref, st_ref), n_ml))
    _interleave(streams)


def _mixers(z, g_rows, g_cols, bias, ret_cos, ret_sin, lg, ret_gn, ml_norm, mb_cos, mb_sin, qw, kw, l, B, S):
    T = B * S
    N = S // ML_CHUNK
    HP = ML_HEADS // 2
    W2 = 2 * LANES
    wide = lambda cb: pl.BlockSpec((S, W2), lambda b, p: (b, cb // 2 + p))
    narrow = lambda cb: pl.BlockSpec((S, LANES), lambda b, p: (b, cb + p))
    table = pl.BlockSpec((S, LANES), lambda b, p: (0, 0))
    pair_w = pl.BlockSpec((None, 1, W2), lambda b, p: (l, 0, p))
    head_w = pl.BlockSpec((None, 1, LANES), lambda b, p: (l, 0, 0))
    out = pl.BlockSpec((S, W2), lambda b, p: (b, p))
    y_shape = jax.ShapeDtypeStruct((T, ML_HEADS * ML_DV), BF16)
    return pl.pallas_call(
        _mixers_kernel,
        grid=(B, HP),
        in_specs=[
            wide(CB_RQ), wide(CB_RK), wide(CB_RV), wide(CB_RG), table, table,
            pl.BlockSpec((2, 1, LANES), lambda b, p: (p, 0, 0)), pair_w,
            narrow(CB_MQ), narrow(CB_MK), wide(CB_MV), wide(CB_MO),
            pl.BlockSpec((None, 2, N, ML_CHUNK), lambda b, p: (b, p, 0, 0)),
            pl.BlockSpec((None, 2, N, ML_CHUNK), lambda b, p: (b, HP + p, 0, 0)),
            pl.BlockSpec((None, 2, ML_CHUNK, N), lambda b, p: (b, p, 0, 0)),
            pl.BlockSpec((None, 2, ML_CHUNK, N), lambda b, p: (b, HP + p, 0, 0)),
            pl.BlockSpec((2, 1, LANES), lambda b, p: (p, 0, 0)),
            pl.BlockSpec((2, 1, LANES), lambda b, p: (HP + p, 0, 0)),
            pair_w,
            wide(CB_AQ), wide(CB_AK), wide(CB_AV), table, table, head_w, head_w,
        ],
        out_specs=[out, out, out],
        out_shape=[y_shape, y_shape, y_shape],
        scratch_shapes=[
            pltpu.VMEM((2 * N, ML_DK, W2), F32),
            pltpu.VMEM((2 * N, ML_DK, W2), BF16),
            pltpu.VMEM((2, S, LANES), BF16),
            pltpu.VMEM((2, LANES, S), BF16),
        ],
        compiler_params=_cparams("parallel", "parallel"),
        name="mixers",
    )(z, z, z, z, ret_cos, ret_sin, lg, ret_gn,
      z, z, z, z, g_rows, g_rows, g_cols, g_cols, bias, bias, ml_norm,
      z, z, z, mb_cos, mb_sin, qw, kw)


def _merge_kernel(ya_ref, yb_ref, yc_ref, ga_ref, gb_ref, gc_ref, wa_ref, wb_ref, wc_ref, wo_ref,
                  x_ref, o_ref, *, rb):
    tm = x_ref.shape[0]

    @pl.when(pl.program_id(1) == 0)
    def _():
        o_ref[...] = x_ref[...]

    def up(k):
        rows = pl.ds(k * rb, rb)
        return tuple(jnp.dot(y_ref[rows, :], w_ref[...], preferred_element_type=F32)
                     for y_ref, w_ref in ((ya_ref, wa_ref), (yb_ref, wb_ref), (yc_ref, wc_ref)))

    def mix(k, prods):
        rows = pl.ds(k * rb, rb)
        m = sum(_sigmoid(g_ref[rows, :].astype(F32)) * p
                for g_ref, p in zip((ga_ref, gb_ref, gc_ref), prods))
        return m.astype(BF16)

    def down(k, m):
        rows = pl.ds(k * rb, rb)
        o_ref[rows, :] += jnp.dot(m, wo_ref[...], preferred_element_type=F32)

    nk = tm // rb
    ups, mids = {}, {}
    for k in range(nk + 2):
        if k < nk:
            ups[k] = up(k)
        if 0 <= k - 1 < nk:
            mids[k - 1] = mix(k - 1, ups.pop(k - 1))
        if 0 <= k - 2 < nk:
            down(k - 2, mids.pop(k - 2))


def _merge(ya, yb, yc, z, wpa, wpb, wpc, wout, x2, l, tm, tc):
    T, D = x2.shape
    W = ya.shape[1]
    yblk = pl.BlockSpec((tm, W), lambda i, c: (i, 0))

    def gblk(w_in_col):
        blocks = [_zcol(w_in_col + k * tc) // tc for k in range(D // tc)]
        assert all(_zcol(w_in_col + k * tc + tc - 1) == b * tc + tc - 1 for k, b in enumerate(blocks))

        def index(i, c):
            blk = blocks[-1]
            for k in range(len(blocks) - 2, -1, -1):
                blk = jnp.where(c == k, blocks[k], blk)
            return (i, blk)

        return pl.BlockSpec((tm, tc), index)

    wblk = pl.BlockSpec((None, W, tc), lambda i, c: (l, 0, c))
    return pl.pallas_call(
        functools.partial(_merge_kernel, rb=min(RB_MERGE, tm)),
        grid=(T // tm, D // tc),
        in_specs=[
            yblk, yblk, yblk, gblk(_O_GA), gblk(_O_GB), gblk(_O_GC), wblk, wblk, wblk,
            pl.BlockSpec((None, tc, D), lambda i, c: (l, c, 0)),
            pl.BlockSpec((tm, D), lambda i, c: (i, 0)),
        ],
        out_specs=pl.BlockSpec((tm, D), lambda i, c: (i, 0)),
        out_shape=jax.ShapeDtypeStruct((T, D), F32),
        compiler_params=_cparams("parallel", "arbitrary"),
        name="merge",
    )(ya, yb, yc, z, z, z, wpa, wpb, wpc, wout, x2)


def _ffn_kernel(x_ref, g_ref, wa_ref, wb_ref, cwa_ref, cwb_ref, cba_ref, cbb_ref, wd_ref,
                o_ref, h_ref, ta_ref, tb_ref, *, tiles_per_seq, rb):
    i = pl.program_id(0)
    c = pl.program_id(1)
    tm = x_ref.shape[0]
    tc = wa_ref.shape[1]

    @pl.when(c == 0)
    def _():
        x = x_ref[...]
        h_ref[...] = _rms(x, g_ref[...], EPS).astype(BF16)
        o_ref[...] = x

    @pl.when((i % tiles_per_seq) == 0)
    def _():
        ta_ref[c] = jnp.zeros((8, tc), F32)
        tb_ref[c] = jnp.zeros((8, tc), F32)

    row = lax.broadcasted_iota(jnp.int32, (rb, tc), 0)
    cwa = cwa_ref[...]
    cwb = cwb_ref[...]
    cba = cba_ref[...]
    cbb = cbb_ref[...]

    def up(k):
        hs = h_ref[pl.ds(k * rb, rb), :]
        return (jnp.dot(hs, wa_ref[...], preferred_element_type=F32),
                jnp.dot(hs, wb_ref[...], preferred_element_type=F32))

    def conv(u, prev, cw, cb):
        p1 = prev[7:8, :]
        p2 = prev[6:7, :]
        u1 = jnp.where(row == 0, p1, pltpu.roll(u, 1, 0))
        u2 = jnp.where(row == 0, p2, jnp.where(row == 1, p1, pltpu.roll(u, 2, 0)))
        return cb + u2 * cw[0:1, :] + u1 * cw[1:2, :] + u * cw[2:3, :]

    def gate(ua, ub, pa, pb):
        a = conv(ua, pa, cwa, cba)
        b = conv(ub, pb, cwb, cbb)
        return (a * _sigmoid(a) * b).astype(BF16)

    def down(k, act):
        rows = pl.ds(k * rb, rb)
        o_ref[rows, :] += jnp.dot(act, wd_ref[...], preferred_element_type=F32)

    pa, pb = ta_ref[c], tb_ref[c]
    nk = tm // rb
    ups, acts = {}, {}
    for k in range(nk + 2):
        if k < nk:
            ua, ub = up(k)
            ups[k] = (ua, ub, pa, pb)
            pa, pb = ua[rb - 8:, :], ub[rb - 8:, :]
        if 0 <= k - 1 < nk:
            acts[k - 1] = gate(*ups.pop(k - 1))
        if 0 <= k - 2 < nk:
            down(k - 2, acts.pop(k - 2))
    ta_ref[c] = pa
    tb_ref[c] = pb


def _ffn(x2, norm_w, w_up, conv_w, conv_b, w_down, l, S, tm, tc):
    T, D = x2.shape
    F = w_down.shape[1]
    nc = F // tc
    wup = lambda off: pl.BlockSpec((None, D, tc), lambda i, c: (l, 0, off + c))
    cw = lambda off: pl.BlockSpec((None, FFN_CONV, tc), lambda i, c: (l, 0, off + c))
    cb = lambda off: pl.BlockSpec((None, 1, tc), lambda i, c: (l, 0, off + c))
    return pl.pallas_call(
        functools.partial(_ffn_kernel, tiles_per_seq=S // tm, rb=min(RB_FFN, tm)),
        grid=(T // tm, nc),
        in_specs=[
            pl.BlockSpec((tm, D), lambda i, c: (i, 0)),
            pl.BlockSpec((None, 1, D), lambda i, c: (l, 0, 0)),
            wup(0), wup(nc), cw(0), cw(nc), cb(0), cb(nc),
            pl.BlockSpec((None, tc, D), lambda i, c: (l, c, 0)),
        ],
        out_specs=pl.BlockSpec((tm, D), lambda i, c: (i, 0)),
        out_shape=jax.ShapeDtypeStruct((T, D), F32),
        scratch_shapes=[
            pltpu.VMEM((tm, D), BF16),
            pltpu.VMEM((nc, 8, tc), F32),
            pltpu.VMEM((nc, 8, tc), F32),
        ],
        compiler_params=_cparams("arbitrary", "arbitrary"),
        name="conv_glu_ffn",
    )(x2, norm_w, w_up, w_up, conv_w, conv_w, conv_b, conv_b, w_down)


def _rot_tables(S, inv_freq):
    ang = jnp.arange(S, dtype=F32)[:, None] * inv_freq[None, :]
    cos = jnp.cos(ang)
    sin = jnp.sin(ang)
    return jnp.concatenate([cos, cos], axis=1), jnp.concatenate([-sin, sin], axis=1)


def _prep_w_in(w_in):
    w_t = jnp.swapaxes(w_in, 1, 2).astype(BF16)
    w_gate = jnp.pad(w_t[:, GATE_COL:GATE_COL + N_GATE], ((0, 0), (0, LANES - N_GATE), (0, 0)))
    return w_t, w_gate


TM_PROJ = 1024
TM_MERGE, TC_MERGE = 512, 1024
RB_MERGE = 256
TM_FFN, TC_FFN = 1024, 512
RB_FFN = 256


def kernel(x, w_in, b_ig, b_fg, norm_mix, ret_gn, ml_norm, q_norm, k_norm, w_pa, w_pb, w_pc, w_out,
           norm_ffn, w_up, conv_w, conv_b, w_down):
    B, S, D = x.shape
    T = B * S
    L = w_in.shape[0]

    w_t, w_gate = _prep_w_in(w_in)
    wpa, wpb, wpc, wout = (w.astype(BF16) for w in (w_pa, w_pb, w_pc, w_out))
    wup = w_up.astype(BF16)
    wdown = w_down.astype(BF16)
    norm_mix3 = norm_mix.reshape(L, 1, D)
    norm_ffn3 = norm_ffn.reshape(L, 1, D)
    ret_gn3 = ret_gn.reshape(L, 1, -1)
    ml_norm3 = ml_norm.reshape(L, 1, -1)
    q_norm3 = q_norm.reshape(L, 1, MB_DH)
    k_norm3 = k_norm.reshape(L, 1, MB_DH)
    conv_b3 = conv_b.reshape(L, 1, -1)

    ret_freq = 1.0 / (ROPE_THETA ** jnp.linspace(0.0, 1.0, RET_DK // 2, dtype=F32))
    rope_freq = 1.0 / (ROPE_THETA ** (jnp.arange(0, MB_DH, 2, dtype=F32) / MB_DH))
    ret_cos, ret_sin = _rot_tables(S, ret_freq)
    mb_cos, mb_sin = _rot_tables(S, rope_freq)
    log_gamma = jnp.log1p(-jnp.exp2(-5.0 - jnp.arange(RET_HEADS, dtype=F32)))
    lg = jnp.broadcast_to(log_gamma[:, None, None], (RET_HEADS, 1, LANES))

    N = S // ML_CHUNK
    x2 = x.reshape(T, D)
    for l in range(L):
        z, zg = _in_proj(x2, norm_mix3, w_t, w_gate, l, min(TM_PROJ, S))
        gates = zg[:, :2 * ML_HEADS].reshape(B, N, ML_CHUNK, 2 * ML_HEADS)
        g_rows = gates.transpose(0, 3, 1, 2)
        g_cols = gates.transpose(0, 3, 2, 1)
        bias = jnp.concatenate([b_ig[l], b_fg[l]])
        bias = jnp.broadcast_to(bias[:, None, None], (2 * ML_HEADS, 1, LANES))
        y_a, y_b, y_c = _mixers(z, g_rows, g_cols, bias, ret_cos, ret_sin, lg, ret_gn3, ml_norm3,
                                mb_cos, mb_sin, q_norm3, k_norm3, l, B, S)
        x2 = _merge(y_a, y_b, y_c, z, wpa, wpb, wpc, wout, x2, l, min(TM_MERGE, S), TC_MERGE)
        x2 = _ffn(x2, norm_ffn3, wup, conv_w, conv_b3, wdown, l, S, min(TM_FFN, S), TC_FFN)
    return x2.reshape(B, S, D)
```

```python
import functools

import jax
import jax.numpy as jnp
from jax import lax
from jax.experimental import pallas as pl
from jax.experimental.pallas import tpu as pltpu

F32 = jnp.float32
BF16 = jnp.bfloat16

D_MODEL = 2048
DEPTH = 4
RET_HEADS = 8
RET_DK = 128
RET_DV = 128
RET_CHUNK = 128
GN_EPS = 1e-5
ML_HEADS = 8
ML_DK = 64
ML_DV = 128
ML_CHUNK = 128
GATE_SOFTCAP = 15.0
MB_HEADS = 8
MB_DH = 128
MB_BLOCK = 256
MB_TOPK = 3
ROPE_THETA = 10000.0
D_FF = 5632
FFN_CONV = 3
EPS = 1e-6

LANES = 128
N_IN = 16400
N_MAIN = 16384
GATE_COL = 7168
N_GATE = 2 * ML_HEADS
WIN_PROJ = 1024
assert GATE_COL % WIN_PROJ == 0 and (N_IN - GATE_COL - N_GATE) % WIN_PROJ == 0


def _zcol(c):
    return c if c < GATE_COL else c - N_GATE


(_O_RQ, _O_RK, _O_RV, _O_RG, _O_MQ, _O_MK, _O_MV, _O_MO, _O_AQ, _O_AK, _O_AV, _O_GA, _O_GB, _O_GC) = (
    0, 1024, 2048, 3072, 4096, 4608, 5120, 6144, 7184, 8208, 9232, 10256, 12304, 14352)
CB_RQ, CB_RK, CB_RV, CB_RG = (_zcol(o) // LANES for o in (_O_RQ, _O_RK, _O_RV, _O_RG))
CB_MQ, CB_MK, CB_MV, CB_MO = (_zcol(o) // LANES for o in (_O_MQ, _O_MK, _O_MV, _O_MO))
CB_AQ, CB_AK, CB_AV = (_zcol(o) // LANES for o in (_O_AQ, _O_AK, _O_AV))
NEG_BIG = -1e30
LOG2E = 1.4426950408889634

VMEM_LIMIT_BYTES = 56 * 1024 * 1024


def _cparams(*sem):
    return pltpu.CompilerParams(dimension_semantics=sem, vmem_limit_bytes=VMEM_LIMIT_BYTES)


def _rms(x, g, eps):
    return x * lax.rsqrt(jnp.mean(x * x, axis=-1, keepdims=True) + eps) * g


def _sigmoid(x):
    return 1.0 / (1.0 + jnp.exp(-x))


def _rot(x, cos, sin_signed):
    return x * cos + pltpu.roll(x, LANES // 2, 1) * sin_signed


_NT = (((1,), (1,)), ((), ()))


def _inproj_kernel(*refs, n_cast):
    x_ref, g_ref, w0_ref, w1_ref, wg_ref = refs[:5]
    src_refs = refs[5:5 + n_cast]
    z_ref, zg_ref = refs[5 + n_cast:7 + n_cast]
    dst_refs = refs[7 + n_cast:7 + 2 * n_cast]
    h_ref = refs[7 + 2 * n_cast]

    @pl.when(pl.program_id(1) == 0)
    def _():
        hb = _rms(x_ref[...], g_ref[...], EPS).astype(BF16)
        h_ref[...] = hb
        zg_ref[...] = lax.dot_general(hb, wg_ref[...], _NT, preferred_element_type=F32)

    for src_ref, dst_ref in zip(src_refs, dst_refs):
        dst_ref[...] = src_ref[...].astype(BF16)

    h = h_ref[...]
    z_ref[:, :WIN_PROJ] = lax.dot_general(h, w0_ref[0], _NT, preferred_element_type=F32).astype(BF16)
    z_ref[:, WIN_PROJ:] = lax.dot_general(h, w1_ref[0], _NT, preferred_element_type=F32).astype(BF16)


BF16_SUBLANES = 16


def _slab_rows(rows, n_steps):
    r = BF16_SUBLANES
    while rows % r or rows // r > n_steps:
        r += BF16_SUBLANES
    return r


def _in_proj(x2, norm_w, w_t, w_gate, cast_ws, l, tm):
    T, D = x2.shape
    grid = (T // tm, N_MAIN // (2 * WIN_PROJ))
    n_steps = grid[0] * grid[1]

    def window(half):
        def index(i, j):
            start = (2 * j + half) * WIN_PROJ
            return (l, pl.multiple_of(jnp.where(start < GATE_COL, start, start + N_GATE), N_GATE), 0)

        return pl.BlockSpec((pl.Element(1), pl.Element(WIN_PROJ), pl.Element(D)), index)

    cast_in, cast_out, cast_shapes = [], [], []
    for w in cast_ws:
        _, rows, cols = w.shape
        sr = _slab_rows(rows, n_steps)
        n_blk = rows // sr
        slab = lambda i, j, n_blk=n_blk: jnp.minimum(i * grid[1] + j, n_blk - 1)
        cast_in.append(pl.BlockSpec((None, sr, cols), lambda i, j, slab=slab: (l, slab(i, j), 0)))
        cast_out.append(pl.BlockSpec((None, sr, cols), lambda i, j, slab=slab: (0, slab(i, j), 0)))
        cast_shapes.append(jax.ShapeDtypeStruct((1, rows, cols), BF16))

    return pl.pallas_call(
        functools.partial(_inproj_kernel, n_cast=len(cast_ws)),
        grid=grid,
        in_specs=[
            pl.BlockSpec((tm, D), lambda i, j: (i, 0)),
            pl.BlockSpec((None, 1, D), lambda i, j: (l, 0, 0)),
            window(0), window(1),
            pl.BlockSpec((None, LANES, D), lambda i, j: (l, 0, 0)),
            *cast_in,
        ],
        out_specs=[
            pl.BlockSpec((tm, 2 * WIN_PROJ), lambda i, j: (i, j)),
            pl.BlockSpec((tm, LANES), lambda i, j: (i, 0)),
            *cast_out,
        ],
        out_shape=[jax.ShapeDtypeStruct((T, N_MAIN), BF16), jax.ShapeDtypeStruct((T, LANES), F32), *cast_shapes],
        scratch_shapes=[pltpu.VMEM((tm, D), BF16)],
        compiler_params=_cparams("arbitrary", "arbitrary"),
        name="in_proj",
    )(x2, norm_w, w_t, w_t, w_gate, *cast_ws)


def _ret_head(q_ref, k_ref, v_ref, g_ref, cos_ref, sin_ref, lg, gn, o_ref, tile):
    S = q_ref.shape[0]
    C = RET_CHUNK
    ii = lax.broadcasted_iota(jnp.int32, (C, C), 0)
    jj = lax.broadcasted_iota(jnp.int32, (C, C), 1)
    decay = jnp.where(ii >= jj, jnp.exp(lg * jnp.maximum(ii - jj, 0).astype(F32)), 0.0)
    pos = lax.broadcasted_iota(jnp.int32, (C, LANES), 0).astype(F32)
    kdec = jnp.exp((C - 1.0 - pos) * lg)
    qdec = jnp.exp((pos + 1.0) * lg)
    cdec = jnp.exp(float(C) * lg)

    state = jnp.zeros((RET_DK, RET_DV), F32)
    for n in range(S // C):
        r = pl.ds(n * C, C)
        cs = cos_ref[r, :]
        sn = sin_ref[r, :]
        q = _rot(q_ref[r, tile].astype(F32), cs, sn)
        k = _rot(k_ref[r, tile].astype(F32), cs, sn) * (RET_DK ** -0.5)
        v = v_ref[r, tile]
        s = lax.dot_general(q.astype(BF16), k.astype(BF16), _NT, preferred_element_type=F32) * decay
        intra = jnp.dot(s.astype(BF16), v, preferred_element_type=F32)
        inter = jnp.dot((q * qdec).astype(BF16), state.astype(BF16), preferred_element_type=F32)
        kv = lax.dot_general((k * kdec).astype(BF16), v, (((0,), (0,)), ((), ())),
                             preferred_element_type=F32)
        o = intra + inter
        oc = o - jnp.mean(o, axis=-1, keepdims=True)
        on = oc * lax.rsqrt(jnp.mean(oc * oc, axis=-1, keepdims=True) + GN_EPS) * gn
        g = g_ref[r, tile].astype(F32)
        o_ref[r, tile] = (g * _sigmoid(g) * on).astype(BF16)
        state = state * cdec + kv
        yield


def _softcap(x):
    return GATE_SOFTCAP * jnp.tanh(x * (1.0 / GATE_SOFTCAP))


def _log_sigmoid(x):
    return jnp.minimum(x, 0.0) - jnp.log(1.0 + jnp.exp(-jnp.abs(x)))


def _cumsum_pow2(x, axis):
    n = x.shape[axis]
    idx = lax.broadcasted_iota(jnp.int32, x.shape, axis)
    s = 1
    while s < n:
        x = x + jnp.where(idx >= s, pltpu.roll(x, s, axis), 0.0)
        s *= 2
    return x


def _cummax_pow2(x, axis):
    n = x.shape[axis]
    idx = lax.broadcasted_iota(jnp.int32, x.shape, axis)
    s = 1
    while s < n:
        x = jnp.maximum(x, jnp.where(idx >= s, pltpu.roll(x, s, axis), NEG_BIG))
        s *= 2
    return x


def _mlstm_pair(q_ref, k_ref, v_ref, op_ref, ir_ref, fr_ref, ic_ref, fc_ref, bi_ref, bf_ref, nw_ref, o_ref,
                kv_ref, st_ref):
    S = q_ref.shape[0]
    C = ML_CHUNK
    N = S // C
    DK = ML_DK
    lane = lax.broadcasted_iota(jnp.int32, (1, LANES), 1)
    lane_n = lax.broadcasted_iota(jnp.int32, (1, N), 1)
    ii = lax.broadcasted_iota(jnp.int32, (C, C), 0)
    jj = lax.broadcasted_iota(jnp.int32, (C, C), 1)
    causal = ii >= jj
    ones_blk = jnp.ones((C, LANES), BF16)
    zero_rows = jnp.zeros((DK, 2 * LANES), BF16)

    for hh in range(2):
        tile = slice(hh * LANES, (hh + 1) * LANES)
        half = ((lane >= hh * DK) & (lane < (hh + 1) * DK)).astype(BF16)
        bi = bi_ref[hh]
        bfg = bf_ref[hh]
        i_r = _softcap(ir_ref[hh] + bi)
        f_r = _log_sigmoid(_softcap(fr_ref[hh] + bfg))
        i_c = _softcap(ic_ref[hh] + bi[:, :N])
        f_c = _log_sigmoid(_softcap(fc_ref[hh] + bfg[:, :N]))
        a_r = _cumsum_pow2(f_r, 1)
        a_c = _cumsum_pow2(f_c, 0)
        b_r = i_r - a_r
        b_c = i_c - a_c
        a_last = a_c[C - 1:C, :]
        g_loc = jnp.max(a_last + b_c, axis=0, keepdims=True)
        w_end_r = a_r[:, C - 1:C] + b_r
        w_exp_r = jnp.exp(w_end_r - jnp.max(w_end_r, axis=1, keepdims=True)) * (DK ** -0.5)

        m_st = jnp.zeros((1, 1), F32)
        m_prev = jnp.zeros((1, N), F32)
        for n in range(N):
            m_prev = jnp.where(lane_n == n, m_st, m_prev)
            m_st = jnp.maximum(a_last[:, n:n + 1] + m_st, g_loc[:, n:n + 1])
        m_next = jnp.maximum(a_last + m_prev, g_loc)
        s_old = jnp.exp(a_last + m_prev - m_next)
        s_new = jnp.exp(g_loc - m_next)
        m_all = jnp.maximum(_cummax_pow2(b_c, 0), m_prev)
        am_all = a_c + m_all
        yield

        for n in range(N):
            r = pl.ds(n * C, C)
            v_aug = jnp.concatenate([v_ref[r, tile], ones_blk], axis=1)
            kwt = (k_ref[r, :].T.astype(F32) * w_exp_r[n:n + 1, :]).astype(BF16)
            kv = jnp.dot(kwt, v_aug, preferred_element_type=F32)
            kv_ref[hh * N + n] = kv[hh * DK:(hh + 1) * DK, :]
            yield

        state = jnp.zeros((DK, 2 * LANES), F32)
        for n in range(N):
            st_ref[hh * N + n] = state.astype(BF16)
            state = s_old[:, n:n + 1] * state + s_new[:, n:n + 1] * kv_ref[hh * N + n]
        yield

        nw = nw_ref[:, tile]
        for n in range(N):
            r = pl.ds(n * C, C)
            q2 = q_ref[r, :]
            v_aug = jnp.concatenate([v_ref[r, tile], ones_blk], axis=1)
            s = lax.dot_general(q2 * half, k_ref[r, :], _NT, preferred_element_type=F32) * (DK ** -0.5)
            m_bc = jnp.broadcast_to(m_all[:, n:n + 1], (C, LANES))
            am_bc = jnp.broadcast_to(am_all[:, n:n + 1], (C, LANES))
            qkw = s * jnp.where(causal, jnp.exp(b_r[n:n + 1, :] - m_bc), 0.0)
            st = st_ref[hh * N + n]
            st_full = jnp.concatenate([st, zero_rows] if hh == 0 else [zero_rows, st], axis=0)
            intra = jnp.dot(qkw.astype(BF16), v_aug, preferred_element_type=F32)
            inter = jnp.dot(q2, st_full, preferred_element_type=F32)
            s_inter = jnp.exp(m_prev[:, n:n + 1] - m_bc)
            num = intra[:, :LANES] + inter[:, :LANES] * s_inter
            den = intra[:, LANES:] + inter[:, LANES:] * s_inter
            den = jnp.maximum(jnp.abs(den), jnp.exp(-am_bc))
            hid = _rms(num / den, nw, EPS)
            o_ref[r, tile] = (_sigmoid(op_ref[r, tile].astype(F32)) * hid).astype(BF16)
            yield


def _moba_head(q_ref, k_ref, v_ref, cos_ref, sin_ref, qw, kw, o_ref, kn_ref, vt_ref, tile):
    S = k_ref.shape[0]
    BLK = MB_BLOCK
    NB = S // BLK

    kmeans = []
    for jb in range(NB):
        r = pl.ds(jb * BLK, BLK)
        kn = _rot(_rms(k_ref[r, tile].astype(F32), kw, EPS), cos_ref[r, :], sin_ref[r, :])
        kn_ref[r, :] = kn.astype(BF16)
        kmeans.append(jnp.mean(kn, axis=0, keepdims=True))
        vt_ref[:, r] = v_ref[r, tile].astype(F32).T.astype(BF16)
        yield
    kmean = jnp.concatenate(kmeans, axis=0)

    blk_id = lax.broadcasted_iota(jnp.int32, (NB, BLK), 0)
    kpos = lax.broadcasted_iota(jnp.int32, (BLK, BLK), 0)
    qpos = lax.broadcasted_iota(jnp.int32, (BLK, BLK), 1)
    causal = kpos <= qpos

    for qi in range(NB):
        rq = pl.ds(qi * BLK, BLK)
        qn = _rot(_rms(q_ref[rq, tile].astype(F32), qw, EPS), cos_ref[rq, :], sin_ref[rq, :])
        qs = (qn * (MB_DH ** -0.5 * LOG2E)).astype(BF16)
        s_all = lax.dot_general(kn_ref[pl.ds(0, (qi + 1) * BLK), :], qs, _NT, preferred_element_type=F32)
        s_own = jnp.where(causal, s_all[qi * BLK:, :], NEG_BIG)
        m = jnp.max(s_own, axis=0, keepdims=True)
        if qi > 0:
            gate = lax.dot_general(kmean, qn, _NT, precision=lax.Precision.HIGHEST, preferred_element_type=F32)
            past = blk_id < qi
            chosen = []
            for jb in range(qi):
                gj = gate[jb:jb + 1, :]
                beats = past & ((gate > gj) | ((gate == gj) & (blk_id < jb)))
                rank = jnp.sum(jnp.where(beats, 1.0, 0.0), axis=0, keepdims=True)
                chosen.append(rank < float(MB_TOPK))
            for jb in range(qi):
                cmax = jnp.max(s_all[jb * BLK:(jb + 1) * BLK, :], axis=0, keepdims=True)
                m = jnp.maximum(m, jnp.where(chosen[jb], cmax, NEG_BIG))
        ps = []
        l = jnp.zeros((1, BLK), F32)
        for jb in range(qi):
            pj = jnp.exp2(s_all[jb * BLK:(jb + 1) * BLK, :] - jnp.where(chosen[jb], m, -NEG_BIG))
            l = l + jnp.sum(pj, axis=0, keepdims=True)
            ps.append(pj.astype(BF16))
        p_own = jnp.exp2(s_own - m)
        l = l + jnp.sum(p_own, axis=0, keepdims=True)
        ps.append(p_own.astype(BF16))
        p_all = jnp.concatenate(ps, axis=0) if qi > 0 else ps[0]
        acc = jnp.dot(vt_ref[:, pl.ds(0, (qi + 1) * BLK)], p_all, preferred_element_type=F32)
        o_ref[rq, tile] = (acc / l).T.astype(BF16)
        yield


def _interleave(streams):
    order = sorted(((k + 0.5) / n, i) for i, (_, n) in enumerate(streams) for k in range(n))
    for _, i in order:
        next(streams[i][0])
    for gen, _ in streams:
        assert next(gen, None) is None


def _mixers_kernel(rq_ref, rk_ref, rv_ref, rg_ref, rcos_ref, rsin_ref, lg_ref, gn_ref,
                   mq_ref, mk_ref, mv_ref, mo_ref, ir_ref, fr_ref, ic_ref, fc_ref, bi_ref, bf_ref, nw_ref,
                   aq_ref, ak_ref, av_ref, acos_ref, asin_ref, qw_ref, kw_ref,
                   ya_ref, yb_ref, yc_ref, kv_ref, st_ref, kn_ref, vt_ref):
    S = rq_ref.shape[0]
    tiles = [slice(0, LANES), slice(LANES, 2 * LANES)]
    qw = qw_ref[...]
    kw = kw_ref[...]
    n_ret = S // RET_CHUNK
    n_ml = 2 * (2 + 2 * (S // ML_CHUNK))
    n_mb = 2 * (S // MB_BLOCK)
    streams = []
    for hd in range(2):
        streams.append((_moba_head(aq_ref, ak_ref, av_ref, acos_ref, asin_ref, qw, kw, yc_ref,
                                   kn_ref.at[hd], vt_ref.at[hd], tiles[hd]), n_mb))
        streams.append((_ret_head(rq_ref, rk_ref, rv_ref, rg_ref, rcos_ref, rsin_ref, lg_ref[hd],
                                  gn_ref[:, tiles[hd]], ya_ref, tiles[hd]), n_ret))
    streams.append((_mlstm_pair(mq_ref, mk_ref, mv_ref, mo_ref, ir_ref, fr_ref, ic_ref, fc_ref, bi_ref, bf_ref,
                                nw_ref, yb_ref, kv_ref, st_ref), n_ml))
    _interleave(streams)


def _mixers(z, g_rows, g_cols, bias, ret_cos, ret_sin, lg, ret_gn, ml_norm, mb_cos, mb_sin, qw, kw, l, B, S):
    T = B * S
    N = S // ML_CHUNK
    HP = ML_HEADS // 2
    W2 = 2 * LANES
    wide = lambda cb: pl.BlockSpec((S, W2), lambda b, p: (b, cb // 2 + p))
    narrow = lambda cb: pl.BlockSpec((S, LANES), lambda b, p: (b, cb + p))
    table = pl.BlockSpec((S, LANES), lambda b, p: (0, 0))
    pair_w = pl.BlockSpec((None, 1, W2), lambda b, p: (l, 0, p))
    head_w = pl.BlockSpec((None, 1, LANES), lambda b, p: (l, 0, 0))
    out = pl.BlockSpec((S, W2), lambda b, p: (b, p))
    y_shape = jax.ShapeDtypeStruct((T, ML_HEADS * ML_DV), BF16)
    return pl.pallas_call(
        _mixers_kernel,
        grid=(B, HP),
        in_specs=[
            wide(CB_RQ), wide(CB_RK), wide(CB_RV), wide(CB_RG), table, table,
            pl.BlockSpec((2, 1, LANES), lambda b, p: (p, 0, 0)), pair_w,
            narrow(CB_MQ), narrow(CB_MK), wide(CB_MV), wide(CB_MO),
            pl.BlockSpec((None, 2, N, ML_CHUNK), lambda b, p: (b, p, 0, 0)),
            pl.BlockSpec((None, 2, N, ML_CHUNK), lambda b, p: (b, HP + p, 0, 0)),
            pl.BlockSpec((None, 2, ML_CHUNK, N), lambda b, p: (b, p, 0, 0)),
            pl.BlockSpec((None, 2, ML_CHUNK, N), lambda b, p: (b, HP + p, 0, 0)),
            pl.BlockSpec((2, 1, LANES), lambda b, p: (p, 0, 0)),
            pl.BlockSpec((2, 1, LANES), lambda b, p: (HP + p, 0, 0)),
            pair_w,
            wide(CB_AQ), wide(CB_AK), wide(CB_AV), table, table, head_w, head_w,
        ],
        out_specs=[out, out, out],
        out_shape=[y_shape, y_shape, y_shape],
        scratch_shapes=[
            pltpu.VMEM((2 * N, ML_DK, W2), F32),
            pltpu.VMEM((2 * N, ML_DK, W2), BF16),
            pltpu.VMEM((2, S, LANES), BF16),
            pltpu.VMEM((2, LANES, S), BF16),
        ],
        compiler_params=_cparams("parallel", "parallel"),
        name="mixers",
    )(z, z, z, z, ret_cos, ret_sin, lg, ret_gn,
      z, z, z, z, g_rows, g_rows, g_cols, g_cols, bias, bias, ml_norm,
      z, z, z, mb_cos, mb_sin, qw, kw)


def _merge_kernel(ya_ref, yb_ref, yc_ref, ga_ref, gb_ref, gc_ref, wa_ref, wb_ref, wc_ref, wo_ref,
                  x_ref, o_ref, *, rb):
    tm = x_ref.shape[0]

    @pl.when(pl.program_id(1) == 0)
    def _():
        o_ref[...] = x_ref[...]

    def up(k):
        rows = pl.ds(k * rb, rb)
        return tuple(jnp.dot(y_ref[rows, :], w_ref[...], preferred_element_type=F32)
                     for y_ref, w_ref in ((ya_ref, wa_ref), (yb_ref, wb_ref), (yc_ref, wc_ref)))

    def mix(k, prods):
        rows = pl.ds(k * rb, rb)
        m = sum(_sigmoid(g_ref[rows, :].astype(F32)) * p
                for g_ref, p in zip((ga_ref, gb_ref, gc_ref), prods))
        return m.astype(BF16)

    def down(k, m):
        rows = pl.ds(k * rb, rb)
        o_ref[rows, :] += jnp.dot(m, wo_ref[...], preferred_element_type=F32)

    nk = tm // rb
    ups, mids = {}, {}
    for k in range(nk + 2):
        if k < nk:
            ups[k] = up(k)
        if 0 <= k - 1 < nk:
            mids[k - 1] = mix(k - 1, ups.pop(k - 1))
        if 0 <= k - 2 < nk:
            down(k - 2, mids.pop(k - 2))


def _merge(ya, yb, yc, z, wpa, wpb, wpc, wout, x2, l, tm, tc):
    T, D = x2.shape
    W = ya.shape[1]
    yblk = pl.BlockSpec((tm, W), lambda i, c: (i, 0))

    def gblk(w_in_col):
        blocks = [_zcol(w_in_col + k * tc) // tc for k in range(D // tc)]
        assert all(_zcol(w_in_col + k * tc + tc - 1) == b * tc + tc - 1 for k, b in enumerate(blocks))

        def index(i, c):
            blk = blocks[-1]
            for k in range(len(blocks) - 2, -1, -1):
                blk = jnp.where(c == k, blocks[k], blk)
            return (i, blk)

        return pl.BlockSpec((tm, tc), index)

    wblk = pl.BlockSpec((None, W, tc), lambda i, c: (l, 0, c))
    return pl.pallas_call(
        functools.partial(_merge_kernel, rb=min(RB_MERGE, tm)),
        grid=(T // tm, D // tc),
        in_specs=[
            yblk, yblk, yblk, gblk(_O_GA), gblk(_O_GB), gblk(_O_GC), wblk, wblk, wblk,
            pl.BlockSpec((None, tc, D), lambda i, c: (l, c, 0)),
            pl.BlockSpec((tm, D), lambda i, c: (i, 0)),
        ],
        out_specs=pl.BlockSpec((tm, D), lambda i, c: (i, 0)),
        out_shape=jax.ShapeDtypeStruct((T, D), F32),
        compiler_params=_cparams("parallel", "arbitrary"),
        name="merge",
    )(ya, yb, yc, z, z, z, wpa, wpb, wpc, wout, x2)


def _ffn_kernel(x_ref, g_ref, wa_ref, wb_ref, cwa_ref, cwb_ref, cba_ref, cbb_ref, wd_ref,
                o_ref, h_ref, ta_ref, tb_ref, *, tiles_per_seq, rb):
    i = pl.program_id(0)
    c = pl.program_id(1)
    tm = x_ref.shape[0]
    tc = wa_ref.shape[1]

    @pl.when(c == 0)
    def _():
        x = x_ref[...]
        h_ref[...] = _rms(x, g_ref[...], EPS).astype(BF16)
        o_ref[...] = x

    @pl.when((i % tiles_per_seq) == 0)
    def _():
        ta_ref[c] = jnp.zeros((8, tc), F32)
        tb_ref[c] = jnp.zeros((8, tc), F32)

    row = lax.broadcasted_iota(jnp.int32, (rb, tc), 0)
    cwa = cwa_ref[...]
    cwb = cwb_ref[...]
    cba = cba_ref[...]
    cbb = cbb_ref[...]

    def up(k):
        hs = h_ref[pl.ds(k * rb, rb), :]
        return (jnp.dot(hs, wa_ref[...], preferred_element_type=F32),
                jnp.dot(hs, wb_ref[...], preferred_element_type=F32))

    def conv(u, prev, cw, cb):
        p1 = prev[7:8, :]
        p2 = prev[6:7, :]
        u1 = jnp.where(row == 0, p1, pltpu.roll(u, 1, 0))
        u2 = jnp.where(row == 0, p2, jnp.where(row == 1, p1, pltpu.roll(u, 2, 0)))
        return cb + u2 * cw[0:1, :] + u1 * cw[1:2, :] + u * cw[2:3, :]

    def gate(ua, ub, pa, pb):
        a = conv(ua, pa, cwa, cba)
        b = conv(ub, pb, cwb, cbb)
        return (a * _sigmoid(a) * b).astype(BF16)

    def down(k, act):
        rows = pl.ds(k * rb, rb)
        o_ref[rows, :] += jnp.dot(act, wd_ref[...], preferred_element_type=F32)

    pa, pb = ta_ref[c], tb_ref[c]
    nk = tm // rb
    ups, acts = {}, {}
    for k in range(nk + 2):
        if k < nk:
            ua, ub = up(k)
            ups[k] = (ua, ub, pa, pb)
            pa, pb = ua[rb - 8:, :], ub[rb - 8:, :]
        if 0 <= k - 1 < nk:
            acts[k - 1] = gate(*ups.pop(k - 1))
        if 0 <= k - 2 < nk:
            down(k - 2, acts.pop(k - 2))
    ta_ref[c] = pa
    tb_ref[c] = pb


def _ffn(x2, norm_w, w_up, conv_w, conv_b, w_down, l, lw, S, tm, tc):
    T, D = x2.shape
    F = w_down.shape[1]
    nc = F // tc
    wup = lambda off: pl.BlockSpec((None, D, tc), lambda i, c: (lw, 0, off + c))
    cw = lambda off: pl.BlockSpec((None, FFN_CONV, tc), lambda i, c: (l, 0, off + c))
    cb = lambda off: pl.BlockSpec((None, 1, tc), lambda i, c: (l, 0, off + c))
    return pl.pallas_call(
        functools.partial(_ffn_kernel, tiles_per_seq=S // tm, rb=min(RB_FFN, tm)),
        grid=(T // tm, nc),
        in_specs=[
            pl.BlockSpec((tm, D), lambda i, c: (i, 0)),
            pl.BlockSpec((None, 1, D), lambda i, c: (l, 0, 0)),
            wup(0), wup(nc), cw(0), cw(nc), cb(0), cb(nc),
            pl.BlockSpec((None, tc, D), lambda i, c: (lw, c, 0)),
        ],
        out_specs=pl.BlockSpec((tm, D), lambda i, c: (i, 0)),
        out_shape=jax.ShapeDtypeStruct((T, D), F32),
        scratch_shapes=[
            pltpu.VMEM((tm, D), BF16),
            pltpu.VMEM((nc, 8, tc), F32),
            pltpu.VMEM((nc, 8, tc), F32),
        ],
        compiler_params=_cparams("arbitrary", "arbitrary"),
        name="conv_glu_ffn",
    )(x2, norm_w, w_up, w_up, conv_w, conv_w, conv_b, conv_b, w_down)


def _rot_tables(S, inv_freq):
    ang = jnp.arange(S, dtype=F32)[:, None] * inv_freq[None, :]
    cos = jnp.cos(ang)
    sin = jnp.sin(ang)
    return jnp.concatenate([cos, cos], axis=1), jnp.concatenate([-sin, sin], axis=1)


def _prep_w_in(w_in):
    w_t = jnp.swapaxes(w_in, 1, 2).astype(BF16)
    w_gate = jnp.pad(w_t[:, GATE_COL:GATE_COL + N_GATE], ((0, 0), (0, LANES - N_GATE), (0, 0)))
    return w_t, w_gate


TM_PROJ = 1024
TM_MERGE, TC_MERGE = 512, 1024
RB_MERGE = 256
TM_FFN, TC_FFN = 1024, 512
RB_FFN = 256


def kernel(x, w_in, b_ig, b_fg, norm_mix, ret_gn, ml_norm, q_norm, k_norm, w_pa, w_pb, w_pc, w_out,
           norm_ffn, w_up, conv_w, conv_b, w_down):
    B, S, D = x.shape
    T = B * S
    L = w_in.shape[0]

    w_t, w_gate = _prep_w_in(w_in)
    norm_mix3 = norm_mix.reshape(L, 1, D)
    norm_ffn3 = norm_ffn.reshape(L, 1, D)
    ret_gn3 = ret_gn.reshape(L, 1, -1)
    ml_norm3 = ml_norm.reshape(L, 1, -1)
    q_norm3 = q_norm.reshape(L, 1, MB_DH)
    k_norm3 = k_norm.reshape(L, 1, MB_DH)
    conv_b3 = conv_b.reshape(L, 1, -1)

    ret_freq = 1.0 / (ROPE_THETA ** jnp.linspace(0.0, 1.0, RET_DK // 2, dtype=F32))
    rope_freq = 1.0 / (ROPE_THETA ** (jnp.arange(0, MB_DH, 2, dtype=F32) / MB_DH))
    ret_cos, ret_sin = _rot_tables(S, ret_freq)
    mb_cos, mb_sin = _rot_tables(S, rope_freq)
    log_gamma = jnp.log1p(-jnp.exp2(-5.0 - jnp.arange(RET_HEADS, dtype=F32)))
    lg = jnp.broadcast_to(log_gamma[:, None, None], (RET_HEADS, 1, LANES))

    N = S // ML_CHUNK
    x2 = x.reshape(T, D)
    for l in range(L):
        z, zg, wpa, wpb, wpc, wout, wup, wdown = _in_proj(
            x2, norm_mix3, w_t, w_gate, (w_pa, w_pb, w_pc, w_out, w_up, w_down), l, min(TM_PROJ, S))
        gates = zg[:, :2 * ML_HEADS].reshape(B, N, ML_CHUNK, 2 * ML_HEADS)
        g_rows = gates.transpose(0, 3, 1, 2)
        g_cols = gates.transpose(0, 3, 2, 1)
        bias = jnp.concatenate([b_ig[l], b_fg[l]])
        bias = jnp.broadcast_to(bias[:, None, None], (2 * ML_HEADS, 1, LANES))
        y_a, y_b, y_c = _mixers(z, g_rows, g_cols, bias, ret_cos, ret_sin, lg, ret_gn3, ml_norm3,
                                mb_cos, mb_sin, q_norm3, k_norm3, l, B, S)
        x2 = _merge(y_a, y_b, y_c, z, wpa, wpb, wpc, wout, x2, 0, min(TM_MERGE, S), TC_MERGE)
        x2 = _ffn(x2, norm_ffn3, wup, conv_w, conv_b3, wdown, l, 0, S, min(TM_FFN, S), TC_FFN)
    return x2.reshape(B, S, D)
```

```python
import functools

import jax
import jax.numpy as jnp
from jax import lax
from jax.experimental import pallas as pl
from jax.experimental.pallas import tpu as pltpu

F32 = jnp.float32
BF16 = jnp.bfloat16

D_MODEL = 2048
DEPTH = 4
RET_HEADS = 8
RET_DK = 128
RET_DV = 128
RET_CHUNK = 128
GN_EPS = 1e-5
ML_HEADS = 8
ML_DK = 64
ML_DV = 128
ML_CHUNK = 128
GATE_SOFTCAP = 15.0
MB_HEADS = 8
MB_DH = 128
MB_BLOCK = 256
MB_TOPK = 3
ROPE_THETA = 10000.0
D_FF = 5632
FFN_CONV = 3
EPS = 1e-6

LANES = 128
N_IN = 16400
N_MAIN = 16384
GATE_COL = 7168
N_GATE = 2 * ML_HEADS
WIN_PROJ = 1024
assert GATE_COL % WIN_PROJ == 0 and (N_IN - GATE_COL - N_GATE) % WIN_PROJ == 0


def _zcol(c):
    return c if c < GATE_COL else c - N_GATE


(_O_RQ, _O_RK, _O_RV, _O_RG, _O_MQ, _O_MK, _O_MV, _O_MO, _O_AQ, _O_AK, _O_AV, _O_GA, _O_GB, _O_GC) = (
    0, 1024, 2048, 3072, 4096, 4608, 5120, 6144, 7184, 8208, 9232, 10256, 12304, 14352)
CB_RQ, CB_RK, CB_RV, CB_RG = (_zcol(o) // LANES for o in (_O_RQ, _O_RK, _O_RV, _O_RG))
CB_MQ, CB_MK, CB_MV, CB_MO = (_zcol(o) // LANES for o in (_O_MQ, _O_MK, _O_MV, _O_MO))
CB_AQ, CB_AK, CB_AV = (_zcol(o) // LANES for o in (_O_AQ, _O_AK, _O_AV))
NEG_BIG = -1e30
LOG2E = 1.4426950408889634

VMEM_LIMIT_BYTES = 56 * 1024 * 1024


def _cparams(*sem):
    return pltpu.CompilerParams(dimension_semantics=sem, vmem_limit_bytes=VMEM_LIMIT_BYTES)


def _rms(x, g, eps):
    return x * lax.rsqrt(jnp.mean(x * x, axis=-1, keepdims=True) + eps) * g


def _sigmoid(x):
    return 1.0 / (1.0 + jnp.exp(-x))


def _rot(x, cos, sin_signed):
    return x * cos + pltpu.roll(x, LANES // 2, 1) * sin_signed


_NT = (((1,), (1,)), ((), ()))


def _inproj_kernel(*refs, n_cast):
    x_ref, g_ref, w0_ref, w1_ref, wg_ref = refs[:5]
    src_refs = refs[5:5 + n_cast]
    z_ref, zg_ref = refs[5 + n_cast:7 + n_cast]
    dst_refs = refs[7 + n_cast:7 + 2 * n_cast]
    h_ref = refs[7 + 2 * n_cast]

    @pl.when(pl.program_id(1) == 0)
    def _():
        hb = _rms(x_ref[...], g_ref[...], EPS).astype(BF16)
        h_ref[...] = hb
        zg_ref[...] = lax.dot_general(hb, wg_ref[...], _NT, preferred_element_type=F32)

    for src_ref, dst_ref in zip(src_refs, dst_refs):
        dst_ref[...] = src_ref[...].astype(BF16)

    h = h_ref[...]
    z_ref[:, :WIN_PROJ] = lax.dot_general(h, w0_ref[0], _NT, preferred_element_type=F32).astype(BF16)
    z_ref[:, WIN_PROJ:] = lax.dot_general(h, w1_ref[0], _NT, preferred_element_type=F32).astype(BF16)


BF16_SUBLANES = 16


def _slab_rows(rows, n_steps):
    r = BF16_SUBLANES
    while rows % r or rows // r > n_steps:
        r += BF16_SUBLANES
    return r


def _in_proj(x2, norm_w, w_t, w_gate, cast_ws, l, tm):
    T, D = x2.shape
    grid = (T // tm, N_MAIN // (2 * WIN_PROJ))
    n_steps = grid[0] * grid[1]

    def window(half):
        def index(i, j):
            start = (2 * j + half) * WIN_PROJ
            return (l, pl.multiple_of(jnp.where(start < GATE_COL, start, start + N_GATE), N_GATE), 0)

        return pl.BlockSpec((pl.Element(1), pl.Element(WIN_PROJ), pl.Element(D)), index)

    cast_in, cast_out, cast_shapes = [], [], []
    for w in cast_ws:
        _, rows, cols = w.shape
        sr = _slab_rows(rows, n_steps)
        n_blk = rows // sr
        slab = lambda i, j, n_blk=n_blk: jnp.minimum(i * grid[1] + j, n_blk - 1)
        cast_in.append(pl.BlockSpec((None, sr, cols), lambda i, j, slab=slab: (l, slab(i, j), 0)))
        cast_out.append(pl.BlockSpec((None, sr, cols), lambda i, j, slab=slab: (0, slab(i, j), 0)))
        cast_shapes.append(jax.ShapeDtypeStruct((1, rows, cols), BF16))

    return pl.pallas_call(
        functools.partial(_inproj_kernel, n_cast=len(cast_ws)),
        grid=grid,
        in_specs=[
            pl.BlockSpec((tm, D), lambda i, j: (i, 0)),
            pl.BlockSpec((None, 1, D), lambda i, j: (l, 0, 0)),
            window(0), window(1),
            pl.BlockSpec((None, LANES, D), lambda i, j: (l, 0, 0)),
            *cast_in,
        ],
        out_specs=[
            pl.BlockSpec((tm, 2 * WIN_PROJ), lambda i, j: (i, j)),
            pl.BlockSpec((tm, LANES), lambda i, j: (i, 0)),
            *cast_out,
        ],
        out_shape=[jax.ShapeDtypeStruct((T, N_MAIN), BF16), jax.ShapeDtypeStruct((T, LANES), F32), *cast_shapes],
        scratch_shapes=[pltpu.VMEM((tm, D), BF16)],
        compiler_params=_cparams("arbitrary", "arbitrary"),
        name="in_proj",
    )(x2, norm_w, w_t, w_t, w_gate, *cast_ws)


def _ret_head(q_ref, k_ref, v_ref, g_ref, cos_ref, sin_ref, lg, gn, o_ref, tile):
    S = q_ref.shape[0]
    C = RET_CHUNK
    ii = lax.broadcasted_iota(jnp.int32, (C, C), 0)
    jj = lax.broadcasted_iota(jnp.int32, (C, C), 1)
    decay = jnp.where(ii >= jj, jnp.exp(lg * jnp.maximum(ii - jj, 0).astype(F32)), 0.0)
    pos = lax.broadcasted_iota(jnp.int32, (C, LANES), 0).astype(F32)
    kdec = jnp.exp((C - 1.0 - pos) * lg)
    qdec = jnp.exp((pos + 1.0) * lg)
    cdec = jnp.exp(float(C) * lg)

    state = jnp.zeros((RET_DK, RET_DV), F32)
    for n in range(S // C):
        r = pl.ds(n * C, C)
        cs = cos_ref[r, :]
        sn = sin_ref[r, :]
        q = _rot(q_ref[r, tile].astype(F32), cs, sn)
        k = _rot(k_ref[r, tile].astype(F32), cs, sn) * (RET_DK ** -0.5)
        v = v_ref[r, tile]
        s = lax.dot_general(q.astype(BF16), k.astype(BF16), _NT, preferred_element_type=F32) * decay
        intra = jnp.dot(s.astype(BF16), v, preferred_element_type=F32)
        inter = jnp.dot((q * qdec).astype(BF16), state.astype(BF16), preferred_element_type=F32)
        kv = lax.dot_general((k * kdec).astype(BF16), v, (((0,), (0,)), ((), ())),
                             preferred_element_type=F32)
        o = intra + inter
        oc = o - jnp.mean(o, axis=-1, keepdims=True)
        on = oc * lax.rsqrt(jnp.mean(oc * oc, axis=-1, keepdims=True) + GN_EPS) * gn
        g = g_ref[r, tile].astype(F32)
        o_ref[r, tile] = (g * _sigmoid(g) * on).astype(BF16)
        state = state * cdec + kv
        yield


def _softcap(x):
    return GATE_SOFTCAP * jnp.tanh(x * (1.0 / GATE_SOFTCAP))


def _log_sigmoid(x):
    return jnp.minimum(x, 0.0) - jnp.log(1.0 + jnp.exp(-jnp.abs(x)))


def _cumsum_pow2(x, axis):
    n = x.shape[axis]
    idx = lax.broadcasted_iota(jnp.int32, x.shape, axis)
    s = 1
    while s < n:
        x = x + jnp.where(idx >= s, pltpu.roll(x, s, axis), 0.0)
        s *= 2
    return x


def _cummax_pow2(x, axis):
    n = x.shape[axis]
    idx = lax.broadcasted_iota(jnp.int32, x.shape, axis)
    s = 1
    while s < n:
        x = jnp.maximum(x, jnp.where(idx >= s, pltpu.roll(x, s, axis), NEG_BIG))
        s *= 2
    return x


def _mlstm_pair(q_ref, k_ref, v_ref, op_ref, ir_ref, fr_ref, ic_ref, fc_ref, bi_ref, bf_ref, nw_ref, o_ref,
                kv_ref, st_ref):
    S = q_ref.shape[0]
    C = ML_CHUNK
    N = S // C
    DK = ML_DK
    lane = lax.broadcasted_iota(jnp.int32, (1, LANES), 1)
    lane_n = lax.broadcasted_iota(jnp.int32, (1, N), 1)
    ii = lax.broadcasted_iota(jnp.int32, (C, C), 0)
    jj = lax.broadcasted_iota(jnp.int32, (C, C), 1)
    causal = ii >= jj
    ones_blk = jnp.ones((C, LANES), BF16)
    zero_rows = jnp.zeros((DK, 2 * LANES), BF16)

    for hh in range(2):
        tile = slice(hh * LANES, (hh + 1) * LANES)
        half = ((lane >= hh * DK) & (lane < (hh + 1) * DK)).astype(BF16)
        bi = bi_ref[hh]
        bfg = bf_ref[hh]
        i_r = _softcap(ir_ref[hh] + bi)
        f_r = _log_sigmoid(_softcap(fr_ref[hh] + bfg))
        i_c = _softcap(ic_ref[hh] + bi[:, :N])
        f_c = _log_sigmoid(_softcap(fc_ref[hh] + bfg[:, :N]))
        a_r = _cumsum_pow2(f_r, 1)
        a_c = _cumsum_pow2(f_c, 0)
        b_r = i_r - a_r
        b_c = i_c - a_c
        a_last = a_c[C - 1:C, :]
        g_loc = jnp.max(a_last + b_c, axis=0, keepdims=True)
        w_end_r = a_r[:, C - 1:C] + b_r
        w_exp_r = jnp.exp(w_end_r - jnp.max(w_end_r, axis=1, keepdims=True)) * (DK ** -0.5)

        m_st = jnp.zeros((1, 1), F32)
        m_prev = jnp.zeros((1, N), F32)
        for n in range(N):
            m_prev = jnp.where(lane_n == n, m_st, m_prev)
            m_st = jnp.maximum(a_last[:, n:n + 1] + m_st, g_loc[:, n:n + 1])
        m_next = jnp.maximum(a_last + m_prev, g_loc)
        s_old = jnp.exp(a_last + m_prev - m_next)
        s_new = jnp.exp(g_loc - m_next)
        m_all = jnp.maximum(_cummax_pow2(b_c, 0), m_prev)
        am_all = a_c + m_all
        yield

        for n in range(N):
            r = pl.ds(n * C, C)
            v_aug = jnp.concatenate([v_ref[r, tile], ones_blk], axis=1)
            kwt = (k_ref[r, :].T.astype(F32) * w_exp_r[n:n + 1, :]).astype(BF16)
            kv = jnp.dot(kwt, v_aug, preferred_element_type=F32)
            kv_ref[hh * N + n] = kv[hh * DK:(hh + 1) * DK, :]
            yield

        state = jnp.zeros((DK, 2 * LANES), F32)
        for n in range(N):
            st_ref[hh * N + n] = state.astype(BF16)
            state = s_old[:, n:n + 1] * state + s_new[:, n:n + 1] * kv_ref[hh * N + n]
        yield

        nw = nw_ref[:, tile]
        for n in range(N):
            r = pl.ds(n * C, C)
            q2 = q_ref[r, :]
            v_aug = jnp.concatenate([v_ref[r, tile], ones_blk], axis=1)
            s = lax.dot_general(q2 * half, k_ref[r, :], _NT, preferred_element_type=F32) * (DK ** -0.5)
            m_bc = jnp.broadcast_to(m_all[:, n:n + 1], (C, LANES))
            am_bc = jnp.broadcast_to(am_all[:, n:n + 1], (C, LANES))
            qkw = s * jnp.where(causal, jnp.exp(b_r[n:n + 1, :] - m_bc), 0.0)
            st = st_ref[hh * N + n]
            st_full = jnp.concatenate([st, zero_rows] if hh == 0 else [zero_rows, st], axis=0)
            intra = jnp.dot(qkw.astype(BF16), v_aug, preferred_element_type=F32)
            inter = jnp.dot(q2, st_full, preferred_element_type=F32)
            s_inter = jnp.exp(m_prev[:, n:n + 1] - m_bc)
            num = intra[:, :LANES] + inter[:, :LANES] * s_inter
            den = intra[:, LANES:] + inter[:, LANES:] * s_inter
            den = jnp.maximum(jnp.abs(den), jnp.exp(-am_bc))
            hid = _rms(num / den, nw, EPS)
            o_ref[r, tile] = (_sigmoid(op_ref[r, tile].astype(F32)) * hid).astype(BF16)
            yield


def _moba_head(q_ref, k_ref, v_ref, cos_ref, sin_ref, qw, kw, o_ref, kn_ref, vt_ref, tile):
    S = k_ref.shape[0]
    BLK = MB_BLOCK
    NB = S // BLK

    kmeans = []
    for jb in range(NB):
        r = pl.ds(jb * BLK, BLK)
        kn = _rot(_rms(k_ref[r, tile].astype(F32), kw, EPS), cos_ref[r, :], sin_ref[r, :])
        kn_ref[r, :] = kn.astype(BF16)
        kmeans.append(jnp.mean(kn, axis=0, keepdims=True))
        vt_ref[:, r] = v_ref[r, tile].astype(F32).T.astype(BF16)
        yield
    kmean = jnp.concatenate(kmeans, axis=0)

    blk_id = lax.broadcasted_iota(jnp.int32, (NB, BLK), 0)
    kpos = lax.broadcasted_iota(jnp.int32, (BLK, BLK), 0)
    qpos = lax.broadcasted_iota(jnp.int32, (BLK, BLK), 1)
    causal = kpos <= qpos

    for qi in range(NB):
        rq = pl.ds(qi * BLK, BLK)
        qn = _rot(_rms(q_ref[rq, tile].astype(F32), qw, EPS), cos_ref[rq, :], sin_ref[rq, :])
        qs = (qn * (MB_DH ** -0.5 * LOG2E)).astype(BF16)
        s_all = lax.dot_general(kn_ref[pl.ds(0, (qi + 1) * BLK), :], qs, _NT, preferred_element_type=F32)
        s_own = jnp.where(causal, s_all[qi * BLK:, :], NEG_BIG)
        m = jnp.max(s_own, axis=0, keepdims=True)
        if qi > 0:
            gate = lax.dot_general(kmean, qn, _NT, precision=lax.Precision.HIGHEST, preferred_element_type=F32)
            past = blk_id < qi
            chosen = []
            for jb in range(qi):
                gj = gate[jb:jb + 1, :]
                beats = past & ((gate > gj) | ((gate == gj) & (blk_id < jb)))
                rank = jnp.sum(jnp.where(beats, 1.0, 0.0), axis=0, keepdims=True)
                chosen.append(rank < float(MB_TOPK))
            for jb in range(qi):
                cmax = jnp.max(s_all[jb * BLK:(jb + 1) * BLK, :], axis=0, keepdims=True)
                m = jnp.maximum(m, jnp.where(chosen[jb], cmax, NEG_BIG))
        ps = []
        l = jnp.zeros((1, BLK), F32)
        for jb in range(qi):
            pj = jnp.exp2(s_all[jb * BLK:(jb + 1) * BLK, :] - jnp.where(chosen[jb], m, -NEG_BIG))
            l = l + jnp.sum(pj, axis=0, keepdims=True)
            ps.append(pj.astype(BF16))
        p_own = jnp.exp2(s_own - m)
        l = l + jnp.sum(p_own, axis=0, keepdims=True)
        ps.append(p_own.astype(BF16))
        p_all = jnp.concatenate(ps, axis=0) if qi > 0 else ps[0]
        acc = jnp.dot(vt_ref[:, pl.ds(0, (qi + 1) * BLK)], p_all, preferred_element_type=F32)
        o_ref[rq, tile] = (acc / l).T.astype(BF16)
        yield


def _interleave(streams):
    order = sorted(((k + 0.5) / n, i) for i, (_, n) in enumerate(streams) for k in range(n))
    for _, i in order:
        next(streams[i][0])
    for gen, _ in streams:
        assert next(gen, None) is None


def _mixers_kernel(rq_ref, rk_ref, rv_ref, rg_ref, rcos_ref, rsin_ref, lg_ref, gn_ref,
                   mq_ref, mk_ref, mv_ref, mo_ref, ir_ref, fr_ref, ic_ref, fc_ref, bi_ref, bf_ref, nw_ref,
                   aq_ref, ak_ref, av_ref, acos_ref, asin_ref, qw_ref, kw_ref,
                   ya_ref, yb_ref, yc_ref, kv_ref, st_ref, kn_ref, vt_ref):
    S = rq_ref.shape[0]
    tiles = [slice(0, LANES), slice(LANES, 2 * LANES)]
    qw = qw_ref[...]
    kw = kw_ref[...]
    n_ret = S // RET_CHUNK
    n_ml = 2 * (2 + 2 * (S // ML_CHUNK))
    n_mb = 2 * (S // MB_BLOCK)
    streams = []
    for hd in range(2):
        streams.append((_moba_head(aq_ref, ak_ref, av_ref, acos_ref, asin_ref, qw, kw, yc_ref,
                                   kn_ref.at[hd], vt_ref.at[hd], tiles[hd]), n_mb))
        streams.append((_ret_head(rq_ref, rk_ref, rv_ref, rg_ref, rcos_ref, rsin_ref, lg_ref[hd],
                                  gn_ref[:, tiles[hd]], ya_ref, tiles[hd]), n_ret))
    streams.append((_mlstm_pair(mq_ref, mk_ref, mv_ref, mo_ref, ir_ref, fr_ref, ic_ref, fc_ref, bi_ref, bf_ref,
                                nw_ref, yb_ref, kv_ref, st_ref), n_ml))
    _interleave(streams)


def _mixers(z, g_rows, g_cols, bias, ret_cos, ret_sin, lg, ret_gn, ml_norm, mb_cos, mb_sin, qw, kw, l, B, S):
    T = B * S
    N = S // ML_CHUNK
    HP = ML_HEADS // 2
    W2 = 2 * LANES
    wide = lambda cb: pl.BlockSpec((S, W2), lambda b, p: (b, cb // 2 + p))
    narrow = lambda cb: pl.BlockSpec((S, LANES), lambda b, p: (b, cb + p))
    table = pl.BlockSpec((S, LANES), lambda b, p: (0, 0))
    pair_w = pl.BlockSpec((None, 1, W2), lambda b, p: (l, 0, p))
    head_w = pl.BlockSpec((None, 1, LANES), lambda b, p: (l, 0, 0))
    out = pl.BlockSpec((S, W2), lambda b, p: (b, p))
    y_shape = jax.ShapeDtypeStruct((T, ML_HEADS * ML_DV), BF16)
    return pl.pallas_call(
        _mixers_kernel,
        grid=(B, HP),
        in_specs=[
            wide(CB_RQ), wide(CB_RK), wide(CB_RV), wide(CB_RG), table, table,
            pl.BlockSpec((2, 1, LANES), lambda b, p: (p, 0, 0)), pair_w,
            narrow(CB_MQ), narrow(CB_MK), wide(CB_MV), wide(CB_MO),
            pl.BlockSpec((None, 2, N, ML_CHUNK), lambda b, p: (b, p, 0, 0)),
            pl.BlockSpec((None, 2, N, ML_CHUNK), lambda b, p: (b, HP + p, 0, 0)),
            pl.BlockSpec((None, 2, ML_CHUNK, N), lambda b, p: (b, p, 0, 0)),
            pl.BlockSpec((None, 2, ML_CHUNK, N), lambda b, p: (b, HP + p, 0, 0)),
            pl.BlockSpec((2, 1, LANES), lambda b, p: (p, 0, 0)),
            pl.BlockSpec((2, 1, LANES), lambda b, p: (HP + p, 0, 0)),
            pair_w,
            wide(CB_AQ), wide(CB_AK), wide(CB_AV), table, table, head_w, head_w,
        ],
        out_specs=[out, out, out],
        out_shape=[y_shape, y_shape, y_shape],
        scratch_shapes=[
            pltpu.VMEM((2 * N, ML_DK, W2), F32),
            pltpu.VMEM((2 * N, ML_DK, W2), BF16),
            pltpu.VMEM((2, S, LANES), BF16),
            pltpu.VMEM((2, LANES, S), BF16),
        ],
        compiler_params=_cparams("parallel", "parallel"),
        name="mixers",
    )(z, z, z, z, ret_cos, ret_sin, lg, ret_gn,
      z, z, z, z, g_rows, g_rows, g_cols, g_cols, bias, bias, ml_norm,
      z, z, z, mb_cos, mb_sin, qw, kw)


def _merge_kernel(ya_ref, yb_ref, yc_ref, ga_ref, gb_ref, gc_ref, wa_ref, wb_ref, wc_ref, wo_ref,
                  x_ref, o_ref, *, rb):
    tm = x_ref.shape[0]

    @pl.when(pl.program_id(1) == 0)
    def _():
        o_ref[...] = x_ref[...]

    def up(k):
        rows = pl.ds(k * rb, rb)
        return tuple(jnp.dot(y_ref[rows, :], w_ref[...], preferred_element_type=F32)
                     for y_ref, w_ref in ((ya_ref, wa_ref), (yb_ref, wb_ref), (yc_ref, wc_ref)))

    def mix(k, prods):
        rows = pl.ds(k * rb, rb)
        m = sum(_sigmoid(g_ref[rows, :].astype(F32)) * p
                for g_ref, p in zip((ga_ref, gb_ref, gc_ref), prods))
        return m.astype(BF16)

    def down(k, m):
        rows = pl.ds(k * rb, rb)
        o_ref[rows, :] += jnp.dot(m, wo_ref[...], preferred_element_type=F32)

    nk = tm // rb
    ups, mids = {}, {}
    for k in range(nk + 2):
        if k < nk:
            ups[k] = up(k)
        if 0 <= k - 1 < nk:
            mids[k - 1] = mix(k - 1, ups.pop(k - 1))
        if 0 <= k - 2 < nk:
            down(k - 2, mids.pop(k - 2))


def _merge(ya, yb, yc, z, wpa, wpb, wpc, wout, x2, l, tm, tc):
    T, D = x2.shape
    W = ya.shape[1]
    yblk = pl.BlockSpec((tm, W), lambda i, c: (i, 0))

    def gblk(w_in_col):
        blocks = [_zcol(w_in_col + k * tc) // tc for k in range(D // tc)]
        assert all(_zcol(w_in_col + k * tc + tc - 1) == b * tc + tc - 1 for k, b in enumerate(blocks))

        def index(i, c):
            blk = blocks[-1]
            for k in range(len(blocks) - 2, -1, -1):
                blk = jnp.where(c == k, blocks[k], blk)
            return (i, blk)

        return pl.BlockSpec((tm, tc), index)

    wblk = pl.BlockSpec((None, W, tc), lambda i, c: (l, 0, c))
    return pl.pallas_call(
        functools.partial(_merge_kernel, rb=min(RB_MERGE, tm)),
        grid=(T // tm, D // tc),
        in_specs=[
            yblk, yblk, yblk, gblk(_O_GA), gblk(_O_GB), gblk(_O_GC), wblk, wblk, wblk,
            pl.BlockSpec((None, tc, D), lambda i, c: (l, c, 0)),
            pl.BlockSpec((tm, D), lambda i, c: (i, 0)),
        ],
        out_specs=pl.BlockSpec((tm, D), lambda i, c: (i, 0)),
        out_shape=jax.ShapeDtypeStruct((T, D), F32),
        compiler_params=_cparams("parallel", "arbitrary"),
        name="merge",
    )(ya, yb, yc, z, z, z, wpa, wpb, wpc, wout, x2)


def _ffn_kernel(x_ref, g_ref, wa_ref, wb_ref, cwa_ref, cwb_ref, cba_ref, cbb_ref, wd_ref,
                o_ref, h_ref, ta_ref, tb_ref, *, tiles_per_seq, rb):
    i = pl.program_id(0)
    c = pl.program_id(1)
    tm = x_ref.shape[0]
    tc = wa_ref.shape[1]

    @pl.when(c == 0)
    def _():
        x = x_ref[...]
        h_ref[...] = _rms(x, g_ref[...], EPS).astype(BF16)
        o_ref[...] = x

    @pl.when((i % tiles_per_seq) == 0)
    def _():
        ta_ref[c] = jnp.zeros((8, tc), F32)
        tb_ref[c] = jnp.zeros((8, tc), F32)

    row = lax.broadcasted_iota(jnp.int32, (rb, tc), 0)
    cwa = cwa_ref[...]
    cwb = cwb_ref[...]
    cba = cba_ref[...]
    cbb = cbb_ref[...]

    def up(k):
        hs = h_ref[pl.ds(k * rb, rb), :]
        return (jnp.dot(hs, wa_ref[...], preferred_element_type=F32),
                jnp.dot(hs, wb_ref[...], preferred_element_type=F32))

    def conv(u, prev, cw, cb):
        p1 = prev[7:8, :]
        p2 = prev[6:7, :]
        u1 = jnp.where(row == 0, p1, pltpu.roll(u, 1, 0))
        u2 = jnp.where(row == 0, p2, jnp.where(row == 1, p1, pltpu.roll(u, 2, 0)))
        return cb + u2 * cw[0:1, :] + u1 * cw[1:2, :] + u * cw[2:3, :]

    def gate(ua, ub, pa, pb):
        a = conv(ua, pa, cwa, cba)
        b = conv(ub, pb, cwb, cbb)
        return (a * _sigmoid(a) * b).astype(BF16)

    def down(k, act):
        rows = pl.ds(k * rb, rb)
        o_ref[rows, :] += jnp.dot(act, wd_ref[...], preferred_element_type=F32)

    pa, pb = ta_ref[c], tb_ref[c]
    nk = tm // rb
    ups, acts = {}, {}
    for k in range(nk + 2):
        if k < nk:
            ua, ub = up(k)
            ups[k] = (ua, ub, pa, pb)
            pa, pb = ua[rb - 8:, :], ub[rb - 8:, :]
        if 0 <= k - 1 < nk:
            acts[k - 1] = gate(*ups.pop(k - 1))
        if 0 <= k - 2 < nk:
            down(k - 2, acts.pop(k - 2))
    ta_ref[c] = pa
    tb_ref[c] = pb


def _ffn(x2, norm_w, w_up, conv_w, conv_b, w_down, l, lw, S, tm, tc):
    T, D = x2.shape
    F = w_down.shape[1]
    nc = F // tc
    wup = lambda off: pl.BlockSpec((None, D, tc), lambda i, c: (lw, 0, off + c))
    cw = lambda off: pl.BlockSpec((None, FFN_CONV, tc), lambda i, c: (l, 0, off + c))
    cb = lambda off: pl.BlockSpec((None, 1, tc), lambda i, c: (l, 0, off + c))
    return pl.pallas_call(
        functools.partial(_ffn_kernel, tiles_per_seq=S // tm, rb=min(RB_FFN, tm)),
        grid=(T // tm, nc),
        in_specs=[
            pl.BlockSpec((tm, D), lambda i, c: (i, 0)),
            pl.BlockSpec((None, 1, D), lambda i, c: (l, 0, 0)),
            wup(0), wup(nc), cw(0), cw(nc), cb(0), cb(nc),
            pl.BlockSpec((None, tc, D), lambda i, c: (lw, c, 0)),
        ],
        out_specs=pl.BlockSpec((tm, D), lambda i, c: (i, 0)),
        out_shape=jax.ShapeDtypeStruct((T, D), F32),
        scratch_shapes=[
            pltpu.VMEM((tm, D), BF16),
            pltpu.VMEM((nc, 8, tc), F32),
            pltpu.VMEM((nc, 8, tc), F32),
        ],
        compiler_params=_cparams("arbitrary", "arbitrary"),
        name="conv_glu_ffn",
    )(x2, norm_w, w_up, w_up, conv_w, conv_w, conv_b, conv_b, w_down)


def _rot_tables(S, inv_freq):
    ang = jnp.arange(S, dtype=F32)[:, None] * inv_freq[None, :]
    cos = jnp.cos(ang)
    sin = jnp.sin(ang)
    return jnp.concatenate([cos, cos], axis=1), jnp.concatenate([-sin, sin], axis=1)


def _prep_w_in(w_in):
    w_t = jnp.swapaxes(w_in, 1, 2).astype(BF16)
    w_gate = jnp.pad(w_t[:, GATE_COL:GATE_COL + N_GATE], ((0, 0), (0, LANES - N_GATE), (0, 0)))
    return w_t, w_gate


TM_PROJ = 1024
TM_MERGE, TC_MERGE = 512, 1024
RB_MERGE = 256
TM_FFN, TC_FFN = 1024, 512
RB_FFN = 512


def kernel(x, w_in, b_ig, b_fg, norm_mix, ret_gn, ml_norm, q_norm, k_norm, w_pa, w_pb, w_pc, w_out,
           norm_ffn, w_up, conv_w, conv_b, w_down):
    B, S, D = x.shape
    T = B * S
    L = w_in.shape[0]

    w_t, w_gate = _prep_w_in(w_in)
    norm_mix3 = norm_mix.reshape(L, 1, D)
    norm_ffn3 = norm_ffn.reshape(L, 1, D)
    ret_gn3 = ret_gn.reshape(L, 1, -1)
    ml_norm3 = ml_norm.reshape(L, 1, -1)
    q_norm3 = q_norm.reshape(L, 1, MB_DH)
    k_norm3 = k_norm.reshape(L, 1, MB_DH)
    conv_b3 = conv_b.reshape(L, 1, -1)

    ret_freq = 1.0 / (ROPE_THETA ** jnp.linspace(0.0, 1.0, RET_DK // 2, dtype=F32))
    rope_freq = 1.0 / (ROPE_THETA ** (jnp.arange(0, MB_DH, 2, dtype=F32) / MB_DH))
    ret_cos, ret_sin = _rot_tables(S, ret_freq)
    mb_cos, mb_sin = _rot_tables(S, rope_freq)
    log_gamma = jnp.log1p(-jnp.exp2(-5.0 - jnp.arange(RET_HEADS, dtype=F32)))
    lg = jnp.broadcast_to(log_gamma[:, None, None], (RET_HEADS, 1, LANES))

    N = S // ML_CHUNK
    x2 = x.reshape(T, D)
    for l in range(L):
        z, zg, wpa, wpb, wpc, wout, wup, wdown = _in_proj(
            x2, norm_mix3, w_t, w_gate, (w_pa, w_pb, w_pc, w_out, w_up, w_down), l, min(TM_PROJ, S))
        gates = zg[:, :2 * ML_HEADS].reshape(B, N, ML_CHUNK, 2 * ML_HEADS)
        g_rows = gates.transpose(0, 3, 1, 2)
        g_cols = gates.transpose(0, 3, 2, 1)
        bias = jnp.concatenate([b_ig[l], b_fg[l]])
        bias = jnp.broadcast_to(bias[:, None, None], (2 * ML_HEADS, 1, LANES))
        y_a, y_b, y_c = _mixers(z, g_rows, g_cols, bias, ret_cos, ret_sin, lg, ret_gn3, ml_norm3,
                                mb_cos, mb_sin, q_norm3, k_norm3, l, B, S)
        x2 = _merge(y_a, y_b, y_c, z, wpa, wpb, wpc, wout, x2, 0, min(TM_MERGE, S), TC_MERGE)
        x2 = _ffn(x2, norm_ffn3, wup, conv_w, conv_b3, wdown, l, 0, S, min(TM_FFN, S), TC_FFN)
    return x2.reshape(B, S, D)
```

```python
import functools

import jax
import jax.numpy as jnp
from jax import lax
from jax.experimental import pallas as pl
from jax.experimental.pallas import tpu as pltpu

F32 = jnp.float32
BF16 = jnp.bfloat16

D_MODEL = 2048
DEPTH = 4
RET_HEADS = 8
RET_DK = 128
RET_DV = 128
RET_CHUNK = 128
GN_EPS = 1e-5
ML_HEADS = 8
ML_DK = 64
ML_DV = 128
ML_CHUNK = 128
GATE_SOFTCAP = 15.0
MB_HEADS = 8
MB_DH = 128
MB_BLOCK = 256
MB_TOPK = 3
ROPE_THETA = 10000.0
D_FF = 5632
FFN_CONV = 3
EPS = 1e-6

LANES = 128
N_IN = 16400
N_MAIN = 16384
GATE_COL = 7168
N_GATE = 2 * ML_HEADS
WIN_PROJ = 1024
assert GATE_COL % WIN_PROJ == 0 and (N_IN - GATE_COL - N_GATE) % WIN_PROJ == 0


def _zcol(c):
    return c if c < GATE_COL else c - N_GATE


(_O_RQ, _O_RK, _O_RV, _O_RG, _O_MQ, _O_MK, _O_MV, _O_MO, _O_AQ, _O_AK, _O_AV, _O_GA, _O_GB, _O_GC) = (
    0, 1024, 2048, 3072, 4096, 4608, 5120, 6144, 7184, 8208, 9232, 10256, 12304, 14352)
CB_RQ, CB_RK, CB_RV, CB_RG = (_zcol(o) // LANES for o in (_O_RQ, _O_RK, _O_RV, _O_RG))
CB_MQ, CB_MK, CB_MV, CB_MO = (_zcol(o) // LANES for o in (_O_MQ, _O_MK, _O_MV, _O_MO))
CB_AQ, CB_AK, CB_AV = (_zcol(o) // LANES for o in (_O_AQ, _O_AK, _O_AV))
NEG_BIG = -1e30
LOG2E = 1.4426950408889634

VMEM_LIMIT_BYTES = 56 * 1024 * 1024


def _cparams(*sem):
    return pltpu.CompilerParams(dimension_semantics=sem, vmem_limit_bytes=VMEM_LIMIT_BYTES)


def _rms(x, g, eps):
    return x * lax.rsqrt(jnp.mean(x * x, axis=-1, keepdims=True) + eps) * g


def _sigmoid(x):
    return 1.0 / (1.0 + jnp.exp(-x))


def _rot(x, cos, sin_signed):
    return x * cos + pltpu.roll(x, LANES // 2, 1) * sin_signed


_NT = (((1,), (1,)), ((), ()))


def _inproj_kernel(*refs, n_cast):
    x_ref, g_ref, w0_ref, w1_ref, wg_ref = refs[:5]
    src_refs = refs[5:5 + n_cast]
    z_ref, zg_ref = refs[5 + n_cast:7 + n_cast]
    dst_refs = refs[7 + n_cast:7 + 2 * n_cast]
    h_ref = refs[7 + 2 * n_cast]

    @pl.when(pl.program_id(1) == 0)
    def _():
        hb = _rms(x_ref[...], g_ref[...], EPS).astype(BF16)
        h_ref[...] = hb
        zg_ref[...] = lax.dot_general(hb, wg_ref[...], _NT, preferred_element_type=F32)

    for src_ref, dst_ref in zip(src_refs, dst_refs):
        dst_ref[...] = src_ref[...].astype(BF16)

    h = h_ref[...]
    z_ref[:, :WIN_PROJ] = lax.dot_general(h, w0_ref[0], _NT, preferred_element_type=F32).astype(BF16)
    z_ref[:, WIN_PROJ:] = lax.dot_general(h, w1_ref[0], _NT, preferred_element_type=F32).astype(BF16)


BF16_SUBLANES = 16


def _slab_rows(rows, n_steps):
    r = BF16_SUBLANES
    while rows % r or rows // r > n_steps:
        r += BF16_SUBLANES
    return r


def _in_proj(x2, norm_w, w_t, w_gate, cast_ws, l, tm):
    T, D = x2.shape
    grid = (T // tm, N_MAIN // (2 * WIN_PROJ))
    n_steps = grid[0] * grid[1]

    def window(half):
        def index(i, j):
            start = (2 * j + half) * WIN_PROJ
            return (l, pl.multiple_of(jnp.where(start < GATE_COL, start, start + N_GATE), N_GATE), 0)

        return pl.BlockSpec((pl.Element(1), pl.Element(WIN_PROJ), pl.Element(D)), index)

    cast_in, cast_out, cast_shapes = [], [], []
    for w in cast_ws:
        _, rows, cols = w.shape
        sr = _slab_rows(rows, n_steps)
        n_blk = rows // sr
        slab = lambda i, j, n_blk=n_blk: jnp.minimum(i * grid[1] + j, n_blk - 1)
        cast_in.append(pl.BlockSpec((None, sr, cols), lambda i, j, slab=slab: (l, slab(i, j), 0)))
        cast_out.append(pl.BlockSpec((None, sr, cols), lambda i, j, slab=slab: (0, slab(i, j), 0)))
        cast_shapes.append(jax.ShapeDtypeStruct((1, rows, cols), BF16))

    return pl.pallas_call(
        functools.partial(_inproj_kernel, n_cast=len(cast_ws)),
        grid=grid,
        in_specs=[
            pl.BlockSpec((tm, D), lambda i, j: (i, 0)),
            pl.BlockSpec((None, 1, D), lambda i, j: (l, 0, 0)),
            window(0), window(1),
            pl.BlockSpec((None, LANES, D), lambda i, j: (l, 0, 0)),
            *cast_in,
        ],
        out_specs=[
            pl.BlockSpec((tm, 2 * WIN_PROJ), lambda i, j: (i, j)),
            pl.BlockSpec((tm, LANES), lambda i, j: (i, 0)),
            *cast_out,
        ],
        out_shape=[jax.ShapeDtypeStruct((T, N_MAIN), BF16), jax.ShapeDtypeStruct((T, LANES), F32), *cast_shapes],
        scratch_shapes=[pltpu.VMEM((tm, D), BF16)],
        compiler_params=_cparams("arbitrary", "arbitrary"),
        name="in_proj",
    )(x2, norm_w, w_t, w_t, w_gate, *cast_ws)


def _ret_head(q_ref, k_ref, v_ref, g_ref, cos_ref, sin_ref, lg, gn, o_ref, tile):
    S = q_ref.shape[0]
    C = RET_CHUNK
    ii = lax.broadcasted_iota(jnp.int32, (C, C), 0)
    jj = lax.broadcasted_iota(jnp.int32, (C, C), 1)
    decay = jnp.where(ii >= jj, jnp.exp(lg * jnp.maximum(ii - jj, 0).astype(F32)), 0.0)
    pos = lax.broadcasted_iota(jnp.int32, (C, LANES), 0).astype(F32)
    kdec = jnp.exp((C - 1.0 - pos) * lg)
    qdec = jnp.exp((pos + 1.0) * lg)
    cdec = jnp.exp(float(C) * lg)

    state = jnp.zeros((RET_DK, RET_DV), F32)
    for n in range(S // C):
        r = pl.ds(n * C, C)
        cs = cos_ref[r, :]
        sn = sin_ref[r, :]
        q = _rot(q_ref[r, tile].astype(F32), cs, sn)
        k = _rot(k_ref[r, tile].astype(F32), cs, sn) * (RET_DK ** -0.5)
        v = v_ref[r, tile]
        s = lax.dot_general(q.astype(BF16), k.astype(BF16), _NT, preferred_element_type=F32) * decay
        intra = jnp.dot(s.astype(BF16), v, preferred_element_type=F32)
        inter = jnp.dot((q * qdec).astype(BF16), state.astype(BF16), preferred_element_type=F32)
        kv = lax.dot_general((k * kdec).astype(BF16), v, (((0,), (0,)), ((), ())),
                             preferred_element_type=F32)
        o = intra + inter
        oc = o - jnp.mean(o, axis=-1, keepdims=True)
        on = oc * lax.rsqrt(jnp.mean(oc * oc, axis=-1, keepdims=True) + GN_EPS) * gn
        g = g_ref[r, tile].astype(F32)
        o_ref[r, tile] = (g * _sigmoid(g) * on).astype(BF16)
        state = state * cdec + kv
        yield


def _softcap(x):
    return GATE_SOFTCAP * jnp.tanh(x * (1.0 / GATE_SOFTCAP))


def _log_sigmoid(x):
    return jnp.minimum(x, 0.0) - jnp.log(1.0 + jnp.exp(-jnp.abs(x)))


def _cumsum_pow2(x, axis):
    n = x.shape[axis]
    idx = lax.broadcasted_iota(jnp.int32, x.shape, axis)
    s = 1
    while s < n:
        x = x + jnp.where(idx >= s, pltpu.roll(x, s, axis), 0.0)
        s *= 2
    return x


def _cummax_pow2(x, axis):
    n = x.shape[axis]
    idx = lax.broadcasted_iota(jnp.int32, x.shape, axis)
    s = 1
    while s < n:
        x = jnp.maximum(x, jnp.where(idx >= s, pltpu.roll(x, s, axis), NEG_BIG))
        s *= 2
    return x


def _mlstm_pair(q_ref, k_ref, v_ref, op_ref, ir_ref, fr_ref, ic_ref, fc_ref, bi_ref, bf_ref, nw_ref, o_ref,
                kv_ref, st_ref):
    S = q_ref.shape[0]
    C = ML_CHUNK
    N = S // C
    DK = ML_DK
    lane = lax.broadcasted_iota(jnp.int32, (1, LANES), 1)
    lane_n = lax.broadcasted_iota(jnp.int32, (1, N), 1)
    ii = lax.broadcasted_iota(jnp.int32, (C, C), 0)
    jj = lax.broadcasted_iota(jnp.int32, (C, C), 1)
    causal = ii >= jj
    ones_blk = jnp.ones((C, LANES), BF16)
    zero_rows = jnp.zeros((DK, 2 * LANES), BF16)

    for hh in range(2):
        tile = slice(hh * LANES, (hh + 1) * LANES)
        half = ((lane >= hh * DK) & (lane < (hh + 1) * DK)).astype(BF16)
        bi = bi_ref[hh]
        bfg = bf_ref[hh]
        i_r = _softcap(ir_ref[hh] + bi)
        f_r = _log_sigmoid(_softcap(fr_ref[hh] + bfg))
        i_c = _softcap(ic_ref[hh] + bi[:, :N])
        f_c = _log_sigmoid(_softcap(fc_ref[hh] + bfg[:, :N]))
        a_r = _cumsum_pow2(f_r, 1)
        a_c = _cumsum_pow2(f_c, 0)
        b_r = i_r - a_r
        b_c = i_c - a_c
        a_last = a_c[C - 1:C, :]
        g_loc = jnp.max(a_last + b_c, axis=0, keepdims=True)
        w_end_r = a_r[:, C - 1:C] + b_r
        w_exp_r = jnp.exp(w_end_r - jnp.max(w_end_r, axis=1, keepdims=True)) * (DK ** -0.5)

        m_st = jnp.zeros((1, 1), F32)
        m_prev = jnp.zeros((1, N), F32)
        for n in range(N):
            m_prev = jnp.where(lane_n == n, m_st, m_prev)
            m_st = jnp.maximum(a_last[:, n:n + 1] + m_st, g_loc[:, n:n + 1])
        m_next = jnp.maximum(a_last + m_prev, g_loc)
        s_old = jnp.exp(a_last + m_prev - m_next)
        s_new = jnp.exp(g_loc - m_next)
        m_all = jnp.maximum(_cummax_pow2(b_c, 0), m_prev)
        am_all = a_c + m_all
        yield

        for n in range(N):
            r = pl.ds(n * C, C)
            v_aug = jnp.concatenate([v_ref[r, tile], ones_blk], axis=1)
            kwt = (k_ref[r, :].T.astype(F32) * w_exp_r[n:n + 1, :]).astype(BF16)
            kv = jnp.dot(kwt, v_aug, preferred_element_type=F32)
            kv_ref[hh * N + n] = kv[hh * DK:(hh + 1) * DK, :]
            yield

        state = jnp.zeros((DK, 2 * LANES), F32)
        for n in range(N):
            st_ref[hh * N + n] = state.astype(BF16)
            state = s_old[:, n:n + 1] * state + s_new[:, n:n + 1] * kv_ref[hh * N + n]
        yield

        nw = nw_ref[:, tile]
        for n in range(N):
            r = pl.ds(n * C, C)
            q2 = q_ref[r, :]
            v_aug = jnp.concatenate([v_ref[r, tile], ones_blk], axis=1)
            s = lax.dot_general(q2 * half, k_ref[r, :], _NT, preferred_element_type=F32) * (DK ** -0.5)
            m_bc = jnp.broadcast_to(m_all[:, n:n + 1], (C, LANES))
            am_bc = jnp.broadcast_to(am_all[:, n:n + 1], (C, LANES))
            qkw = s * jnp.where(causal, jnp.exp(b_r[n:n + 1, :] - m_bc), 0.0)
            st = st_ref[hh * N + n]
            st_full = jnp.concatenate([st, zero_rows] if hh == 0 else [zero_rows, st], axis=0)
            intra = jnp.dot(qkw.astype(BF16), v_aug, preferred_element_type=F32)
            inter = jnp.dot(q2, st_full, preferred_element_type=F32)
            s_inter = jnp.exp(m_prev[:, n:n + 1] - m_bc)
            num = intra[:, :LANES] + inter[:, :LANES] * s_inter
            den = intra[:, LANES:] + inter[:, LANES:] * s_inter
            den = jnp.maximum(jnp.abs(den), jnp.exp(-am_bc))
            hid = _rms(num / den, nw, EPS)
            o_ref[r, tile] = (_sigmoid(op_ref[r, tile].astype(F32)) * hid).astype(BF16)
            yield


def _moba_head(q_ref, k_ref, v_ref, cos_ref, sin_ref, qw, kw, o_ref, kn_ref, vt_ref, tile):
    S = k_ref.shape[0]
    BLK = MB_BLOCK
    NB = S // BLK

    kmeans = []
    for jb in range(NB):
        r = pl.ds(jb * BLK, BLK)
        kn = _rot(_rms(k_ref[r, tile].astype(F32), kw, EPS), cos_ref[r, :], sin_ref[r, :])
        kn_ref[r, :] = kn.astype(BF16)
        kmeans.append(jnp.mean(kn, axis=0, keepdims=True))
        vt_ref[:, r] = v_ref[r, tile].astype(F32).T.astype(BF16)
        yield
    kmean = jnp.concatenate(kmeans, axis=0)

    blk_id = lax.broadcasted_iota(jnp.int32, (NB, BLK), 0)
    kpos = lax.broadcasted_iota(jnp.int32, (BLK, BLK), 0)
    qpos = lax.broadcasted_iota(jnp.int32, (BLK, BLK), 1)
    causal = kpos <= qpos

    for qi in range(NB):
        rq = pl.ds(qi * BLK, BLK)
        qn = _rot(_rms(q_ref[rq, tile].astype(F32), qw, EPS), cos_ref[rq, :], sin_ref[rq, :])
        qs = (qn * (MB_DH ** -0.5 * LOG2E)).astype(BF16)
        s_all = lax.dot_general(kn_ref[pl.ds(0, (qi + 1) * BLK), :], qs, _NT, preferred_element_type=F32)
        s_own = jnp.where(causal, s_all[qi * BLK:, :], NEG_BIG)
        m = jnp.max(s_own, axis=0, keepdims=True)
        if qi > 0:
            gate = lax.dot_general(kmean, qn, _NT, precision=lax.Precision.HIGHEST, preferred_element_type=F32)
            past = blk_id < qi
            chosen = []
            for jb in range(qi):
                gj = gate[jb:jb + 1, :]
                beats = past & ((gate > gj) | ((gate == gj) & (blk_id < jb)))
                rank = jnp.sum(jnp.where(beats, 1.0, 0.0), axis=0, keepdims=True)
                chosen.append(rank < float(MB_TOPK))
            for jb in range(qi):
                cmax = jnp.max(s_all[jb * BLK:(jb + 1) * BLK, :], axis=0, keepdims=True)
                m = jnp.maximum(m, jnp.where(chosen[jb], cmax, NEG_BIG))
        yield
        ps = []
        l = jnp.zeros((1, BLK), F32)
        for jb in range(qi):
            pj = jnp.exp2(s_all[jb * BLK:(jb + 1) * BLK, :] - jnp.where(chosen[jb], m, -NEG_BIG))
            l = l + jnp.sum(pj, axis=0, keepdims=True)
            ps.append(pj.astype(BF16))
        p_own = jnp.exp2(s_own - m)
        l = l + jnp.sum(p_own, axis=0, keepdims=True)
        ps.append(p_own.astype(BF16))
        p_all = jnp.concatenate(ps, axis=0) if qi > 0 else ps[0]
        yield
        acc = jnp.dot(vt_ref[:, pl.ds(0, (qi + 1) * BLK)], p_all, preferred_element_type=F32)
        o_ref[rq, tile] = (acc / l).T.astype(BF16)
        yield


def _interleave(streams):
    order = sorted(((k + 0.5) / n, i) for i, (_, n) in enumerate(streams) for k in range(n))
    for _, i in order:
        next(streams[i][0])
    for gen, _ in streams:
        assert next(gen, None) is None


def _mixers_kernel(rq_ref, rk_ref, rv_ref, rg_ref, rcos_ref, rsin_ref, lg_ref, gn_ref,
                   mq_ref, mk_ref, mv_ref, mo_ref, ir_ref, fr_ref, ic_ref, fc_ref, bi_ref, bf_ref, nw_ref,
                   aq_ref, ak_ref, av_ref, acos_ref, asin_ref, qw_ref, kw_ref,
                   ya_ref, yb_ref, yc_ref, kv_ref, st_ref, kn_ref, vt_ref):
    S = rq_ref.shape[0]
    tiles = [slice(0, LANES), slice(LANES, 2 * LANES)]
    qw = qw_ref[...]
    kw = kw_ref[...]
    n_ret = S // RET_CHUNK
    n_ml = 2 * (2 + 2 * (S // ML_CHUNK))
    n_mb = 4 * (S // MB_BLOCK)
    streams = []
    for hd in range(2):
        streams.append((_moba_head(aq_ref, ak_ref, av_ref, acos_ref, asin_ref, qw, kw, yc_ref,
                                   kn_ref.at[hd], vt_ref.at[hd], tiles[hd]), n_mb))
        streams.append((_ret_head(rq_ref, rk_ref, rv_ref, rg_ref, rcos_ref, rsin_ref, lg_ref[hd],
                                  gn_ref[:, tiles[hd]], ya_ref, tiles[hd]), n_ret))
    streams.append((_mlstm_pair(mq_ref, mk_ref, mv_ref, mo_ref, ir_ref, fr_ref, ic_ref, fc_ref, bi_ref, bf_ref,
                                nw_ref, yb_ref, kv_ref, st_ref), n_ml))
    _interleave(streams)


def _mixers(z, g_rows, g_cols, bias, ret_cos, ret_sin, lg, ret_gn, ml_norm, mb_cos, mb_sin, qw, kw, l, B, S):
    T = B * S
    N = S // ML_CHUNK
    HP = ML_HEADS // 2
    W2 = 2 * LANES
    wide = lambda cb: pl.BlockSpec((S, W2), lambda b, p: (b, cb // 2 + p))
    narrow = lambda cb: pl.BlockSpec((S, LANES), lambda b, p: (b, cb + p))
    table = pl.BlockSpec((S, LANES), lambda b, p: (0, 0))
    pair_w = pl.BlockSpec((None, 1, W2), lambda b, p: (l, 0, p))
    head_w = pl.BlockSpec((None, 1, LANES), lambda b, p: (l, 0, 0))
    out = pl.BlockSpec((S, W2), lambda b, p: (b, p))
    y_shape = jax.ShapeDtypeStruct((T, ML_HEADS * ML_DV), BF16)
    return pl.pallas_call(
        _mixers_kernel,
        grid=(B, HP),
        in_specs=[
            wide(CB_RQ), wide(CB_RK), wide(CB_RV), wide(CB_RG), table, table,
            pl.BlockSpec((2, 1, LANES), lambda b, p: (p, 0, 0)), pair_w,
            narrow(CB_MQ), narrow(CB_MK), wide(CB_MV), wide(CB_MO),
            pl.BlockSpec((None, 2, N, ML_CHUNK), lambda b, p: (b, p, 0, 0)),
            pl.BlockSpec((None, 2, N, ML_CHUNK), lambda b, p: (b, HP + p, 0, 0)),
            pl.BlockSpec((None, 2, ML_CHUNK, N), lambda b, p: (b, p, 0, 0)),
            pl.BlockSpec((None, 2, ML_CHUNK, N), lambda b, p: (b, HP + p, 0, 0)),
            pl.BlockSpec((2, 1, LANES), lambda b, p: (p, 0, 0)),
            pl.BlockSpec((2, 1, LANES), lambda b, p: (HP + p, 0, 0)),
            pair_w,
            wide(CB_AQ), wide(CB_AK), wide(CB_AV), table, table, head_w, head_w,
        ],
        out_specs=[out, out, out],
        out_shape=[y_shape, y_shape, y_shape],
        scratch_shapes=[
            pltpu.VMEM((2 * N, ML_DK, W2), F32),
            pltpu.VMEM((2 * N, ML_DK, W2), BF16),
            pltpu.VMEM((2, S, LANES), BF16),
            pltpu.VMEM((2, LANES, S), BF16),
        ],
        compiler_params=_cparams("parallel", "parallel"),
        name="mixers",
    )(z, z, z, z, ret_cos, ret_sin, lg, ret_gn,
      z, z, z, z, g_rows, g_rows, g_cols, g_cols, bias, bias, ml_norm,
      z, z, z, mb_cos, mb_sin, qw, kw)


def _merge_kernel(ya_ref, yb_ref, yc_ref, ga_ref, gb_ref, gc_ref, wa_ref, wb_ref, wc_ref, wo_ref,
                  x_ref, o_ref, *, rb):
    tm = x_ref.shape[0]

    @pl.when(pl.program_id(1) == 0)
    def _():
        o_ref[...] = x_ref[...]

    def up(k):
        rows = pl.ds(k * rb, rb)
        return tuple(jnp.dot(y_ref[rows, :], w_ref[...], preferred_element_type=F32)
                     for y_ref, w_ref in ((ya_ref, wa_ref), (yb_ref, wb_ref), (yc_ref, wc_ref)))

    def mix(k, prods):
        rows = pl.ds(k * rb, rb)
        m = sum(_sigmoid(g_ref[rows, :].astype(F32)) * p
                for g_ref, p in zip((ga_ref, gb_ref, gc_ref), prods))
        return m.astype(BF16)

    def down(k, m):
        rows = pl.ds(k * rb, rb)
        o_ref[rows, :] += jnp.dot(m, wo_ref[...], preferred_element_type=F32)

    nk = tm // rb
    ups, mids = {}, {}
    for k in range(nk + 2):
        if k < nk:
            ups[k] = up(k)
        if 0 <= k - 1 < nk:
            mids[k - 1] = mix(k - 1, ups.pop(k - 1))
        if 0 <= k - 2 < nk:
            down(k - 2, mids.pop(k - 2))


def _merge(ya, yb, yc, z, wpa, wpb, wpc, wout, x2, l, tm, tc):
    T, D = x2.shape
    W = ya.shape[1]
    yblk = pl.BlockSpec((tm, W), lambda i, c: (i, 0))

    def gblk(w_in_col):
        blocks = [_zcol(w_in_col + k * tc) // tc for k in range(D // tc)]
        assert all(_zcol(w_in_col + k * tc + tc - 1) == b * tc + tc - 1 for k, b in enumerate(blocks))

        def index(i, c):
            blk = blocks[-1]
            for k in range(len(blocks) - 2, -1, -1):
                blk = jnp.where(c == k, blocks[k], blk)
            return (i, blk)

        return pl.BlockSpec((tm, tc), index)

    wblk = pl.BlockSpec((None, W, tc), lambda i, c: (l, 0, c))
    return pl.pallas_call(
        functools.partial(_merge_kernel, rb=min(RB_MERGE, tm)),
        grid=(T // tm, D // tc),
        in_specs=[
            yblk, yblk, yblk, gblk(_O_GA), gblk(_O_GB), gblk(_O_GC), wblk, wblk, wblk,
            pl.BlockSpec((None, tc, D), lambda i, c: (l, c, 0)),
            pl.BlockSpec((tm, D), lambda i, c: (i, 0)),
        ],
        out_specs=pl.BlockSpec((tm, D), lambda i, c: (i, 0)),
        out_shape=jax.ShapeDtypeStruct((T, D), F32),
        compiler_params=_cparams("parallel", "arbitrary"),
        name="merge",
    )(ya, yb, yc, z, z, z, wpa, wpb, wpc, wout, x2)


def _ffn_kernel(x_ref, g_ref, wa_ref, wb_ref, cwa_ref, cwb_ref, cba_ref, cbb_ref, wd_ref,
                o_ref, h_ref, ta_ref, tb_ref, *, tiles_per_seq, rb):
    i = pl.program_id(0)
    c = pl.program_id(1)
    tm = x_ref.shape[0]
    tc = wa_ref.shape[1]

    @pl.when(c == 0)
    def _():
        x = x_ref[...]
        h_ref[...] = _rms(x, g_ref[...], EPS).astype(BF16)
        o_ref[...] = x

    @pl.when((i % tiles_per_seq) == 0)
    def _():
        ta_ref[c] = jnp.zeros((8, tc), F32)
        tb_ref[c] = jnp.zeros((8, tc), F32)

    row = lax.broadcasted_iota(jnp.int32, (rb, tc), 0)
    cwa = cwa_ref[...]
    cwb = cwb_ref[...]
    cba = cba_ref[...]
    cbb = cbb_ref[...]

    def up(k):
        hs = h_ref[pl.ds(k * rb, rb), :]
        return (jnp.dot(hs, wa_ref[...], preferred_element_type=F32),
                jnp.dot(hs, wb_ref[...], preferred_element_type=F32))

    def conv(u, prev, cw, cb):
        p1 = prev[7:8, :]
        p2 = prev[6:7, :]
        u1 = jnp.where(row == 0, p1, pltpu.roll(u, 1, 0))
        u2 = jnp.where(row == 0, p2, jnp.where(row == 1, p1, pltpu.roll(u, 2, 0)))
        return cb + u2 * cw[0:1, :] + u1 * cw[1:2, :] + u * cw[2:3, :]

    def gate(ua, ub, pa, pb):
        a = conv(ua, pa, cwa, cba)
        b = conv(ub, pb, cwb, cbb)
        return (a * _sigmoid(a) * b).astype(BF16)

    def down(k, act):
        rows = pl.ds(k * rb, rb)
        o_ref[rows, :] += jnp.dot(act, wd_ref[...], preferred_element_type=F32)

    pa, pb = ta_ref[c], tb_ref[c]
    nk = tm // rb
    ups, acts = {}, {}
    for k in range(nk + 2):
        if k < nk:
            ua, ub = up(k)
            ups[k] = (ua, ub, pa, pb)
            pa, pb = ua[rb - 8:, :], ub[rb - 8:, :]
        if 0 <= k - 1 < nk:
            acts[k - 1] = gate(*ups.pop(k - 1))
        if 0 <= k - 2 < nk:
            down(k - 2, acts.pop(k - 2))
    ta_ref[c] = pa
    tb_ref[c] = pb


def _ffn(x2, norm_w, w_up, conv_w, conv_b, w_down, l, lw, S, tm, tc):
    T, D = x2.shape
    F = w_down.shape[1]
    nc = F // tc
    wup = lambda off: pl.BlockSpec((None, D, tc), lambda i, c: (lw, 0, off + c))
    cw = lambda off: pl.BlockSpec((None, FFN_CONV, tc), lambda i, c: (l, 0, off + c))
    cb = lambda off: pl.BlockSpec((None, 1, tc), lambda i, c: (l, 0, off + c))
    return pl.pallas_call(
        functools.partial(_ffn_kernel, tiles_per_seq=S // tm, rb=min(RB_FFN, tm)),
        grid=(T // tm, nc),
        in_specs=[
            pl.BlockSpec((tm, D), lambda i, c: (i, 0)),
            pl.BlockSpec((None, 1, D), lambda i, c: (l, 0, 0)),
            wup(0), wup(nc), cw(0), cw(nc), cb(0), cb(nc),
            pl.BlockSpec((None, tc, D), lambda i, c: (lw, c, 0)),
        ],
        out_specs=pl.BlockSpec((tm, D), lambda i, c: (i, 0)),
        out_shape=jax.ShapeDtypeStruct((T, D), F32),
        scratch_shapes=[
            pltpu.VMEM((tm, D), BF16),
            pltpu.VMEM((nc, 8, tc), F32),
            pltpu.VMEM((nc, 8, tc), F32),
        ],
        compiler_params=_cparams("arbitrary", "arbitrary"),
        name="conv_glu_ffn",
    )(x2, norm_w, w_up, w_up, conv_w, conv_w, conv_b, conv_b, w_down)


def _rot_tables(S, inv_freq):
    ang = jnp.arange(S, dtype=F32)[:, None] * inv_freq[None, :]
    cos = jnp.cos(ang)
    sin = jnp.sin(ang)
    return jnp.concatenate([cos, cos], axis=1), jnp.concatenate([-sin, sin], axis=1)


def _prep_w_in(w_in):
    w_t = jnp.swapaxes(w_in, 1, 2).astype(BF16)
    w_gate = jnp.pad(w_t[:, GATE_COL:GATE_COL + N_GATE], ((0, 0), (0, LANES - N_GATE), (0, 0)))
    return w_t, w_gate


TM_PROJ = 1024
TM_MERGE, TC_MERGE = 512, 1024
RB_MERGE = 256
TM_FFN, TC_FFN = 1024, 512
RB_FFN = 512


def kernel(x, w_in, b_ig, b_fg, norm_mix, ret_gn, ml_norm, q_norm, k_norm, w_pa, w_pb, w_pc, w_out,
           norm_ffn, w_up, conv_w, conv_b, w_down):
    B, S, D = x.shape
    T = B * S
    L = w_in.shape[0]

    w_t, w_gate = _prep_w_in(w_in)
    norm_mix3 = norm_mix.reshape(L, 1, D)
    norm_ffn3 = norm_ffn.reshape(L, 1, D)
    ret_gn3 = ret_gn.reshape(L, 1, -1)
    ml_norm3 = ml_norm.reshape(L, 1, -1)
    q_norm3 = q_norm.reshape(L, 1, MB_DH)
    k_norm3 = k_norm.reshape(L, 1, MB_DH)
    conv_b3 = conv_b.reshape(L, 1, -1)

    ret_freq = 1.0 / (ROPE_THETA ** jnp.linspace(0.0, 1.0, RET_DK // 2, dtype=F32))
    rope_freq = 1.0 / (ROPE_THETA ** (jnp.arange(0, MB_DH, 2, dtype=F32) / MB_DH))
    ret_cos, ret_sin = _rot_tables(S, ret_freq)
    mb_cos, mb_sin = _rot_tables(S, rope_freq)
    log_gamma = jnp.log1p(-jnp.exp2(-5.0 - jnp.arange(RET_HEADS, dtype=F32)))
    lg = jnp.broadcast_to(log_gamma[:, None, None], (RET_HEADS, 1, LANES))

    N = S // ML_CHUNK
    x2 = x.reshape(T, D)
    for l in range(L):
        z, zg, wpa, wpb, wpc, wout, wup, wdown = _in_proj(
            x2, norm_mix3, w_t, w_gate, (w_pa, w_pb, w_pc, w_out, w_up, w_down), l, min(TM_PROJ, S))
        gates = zg[:, :2 * ML_HEADS].reshape(B, N, ML_CHUNK, 2 * ML_HEADS)
        g_rows = gates.transpose(0, 3, 1, 2)
        g_cols = gates.transpose(0, 3, 2, 1)
        bias = jnp.concatenate([b_ig[l], b_fg[l]])
        bias = jnp.broadcast_to(bias[:, None, None], (2 * ML_HEADS, 1, LANES))
        y_a, y_b, y_c = _mixers(z, g_rows, g_cols, bias, ret_cos, ret_sin, lg, ret_gn3, ml_norm3,
                                mb_cos, mb_sin, q_norm3, k_norm3, l, B, S)
        x2 = _merge(y_a, y_b, y_c, z, wpa, wpb, wpc, wout, x2, 0, min(TM_MERGE, S), TC_MERGE)
        x2 = _ffn(x2, norm_ffn3, wup, conv_w, conv_b3, wdown, l, 0, S, min(TM_FFN, S), TC_FFN)
    return x2.reshape(B, S, D)
```
